```python
import jax, jax.numpy as jnp
from jax import lax
import numpy as np

D_MODEL = 1024
BATCH = 4
SEQ = 4096
DEPTH = 1

CHUNK = 64
PLE_DIM = 256
EPS = 1e-6

GLA_HEADS = 4
GLA_DK = 128
GLA_DV = 256
GLA_GATE_RANK = 16
GLA_TAU = 16.0

SB_HEADS = 16
SB_DH = 64
SB_BLOCK = 128

PEER_HEADS = 8
PEER_NKEYS = 128
PEER_N = PEER_NKEYS * PEER_NKEYS
PEER_DKEY = 256
PEER_TOPK = 16
PEER_TOK_BLOCK = 128

GLA_QK = GLA_HEADS * GLA_DK
GLA_V = GLA_HEADS * GLA_DV
SB_W = SB_HEADS * SB_DH
IN_SPLITS = (GLA_QK, GLA_QK, GLA_V, GLA_V, GLA_GATE_RANK, SB_W, SB_W, SB_W, D_MODEL, D_MODEL)
IN_WIDTH = sum(IN_SPLITS)

kernel_name = "hybrid_gla_stickbreak_peer_block"


def rmsnorm(x, g):
    xf = x.astype(jnp.float32)
    y = xf * lax.rsqrt(jnp.mean(xf * xf, axis=-1, keepdims=True) + EPS)
    return (y * g.astype(jnp.float32)).astype(x.dtype)


def gla_branch(q, k, v, r, a_lo, w_a_up, b_a, g_o):
    B, S, _ = q.shape
    NC = S // CHUNK
    f32 = jnp.float32
    qf = q.astype(f32).reshape(B, NC, CHUNK, GLA_HEADS, GLA_DK) * (GLA_DK ** -0.5)
    kf = k.astype(f32).reshape(B, NC, CHUNK, GLA_HEADS, GLA_DK)
    vf = v.astype(f32).reshape(B, NC, CHUNK, GLA_HEADS, GLA_DV)
    log_a = jax.nn.log_sigmoid((a_lo @ w_a_up + b_a).astype(f32)) / GLA_TAU
    log_a = log_a.reshape(B, NC, CHUNK, GLA_HEADS, GLA_DK)
    logb = jnp.cumsum(log_a, axis=2)
    k_dec = kf * jnp.exp(logb[:, :, -1:] - logb)
    g_chunk = jnp.exp(logb[:, :, -1])

    def step(state, inp):
        qc, kc, vc, gc = inp
        state = gc[..., None] * state + jnp.einsum('bchk,bchv->bhkv', kc, vc)
        oc = jnp.einsum('bchk,bhkv->bchv', qc, state)
        return state, oc

    xs = (jnp.moveaxis(qf, 1, 0), jnp.moveaxis(k_dec, 1, 0),
          jnp.moveaxis(vf, 1, 0), jnp.moveaxis(g_chunk, 1, 0))
    state0 = jnp.zeros((B, GLA_HEADS, GLA_DK, GLA_DV), f32)
    _, o = lax.scan(step, state0, xs)
    o = jnp.moveaxis(o, 0, 1).reshape(B, S, GLA_HEADS, GLA_DV)
    o = o * lax.rsqrt(jnp.mean(o * o, axis=-1, keepdims=True) + EPS)
    o = o * g_o.astype(f32).reshape(GLA_HEADS, GLA_DV)
    o = o.reshape(B, S, GLA_V) * jax.nn.silu(r.astype(f32))
    return o.astype(q.dtype)


def stick_breaking_branch(q, k, v):
    B, S, _ = q.shape
    NB = S // SB_BLOCK
    f32 = jnp.float32
    qf = q.astype(f32).reshape(B, S, SB_HEADS, SB_DH).transpose(0, 2, 1, 3) * (SB_DH ** -0.5)
    kf = k.astype(f32).reshape(B, S, SB_HEADS, SB_DH).transpose(0, 2, 1, 3)
    vf = v.astype(f32).reshape(B, S, SB_HEADS, SB_DH).transpose(0, 2, 1, 3)
    qb = qf.reshape(B, SB_HEADS, NB, SB_BLOCK, SB_DH).transpose(2, 0, 1, 3, 4)
    starts = jnp.arange(NB, dtype=jnp.int32) * SB_BLOCK
    key_pos = jnp.arange(S, dtype=jnp.int32)

    def block(args):
        qblk, start = args
        z = jnp.einsum('bhqd,bhkd->bhqk', qblk, kf)
        q_pos = start + jnp.arange(SB_BLOCK, dtype=jnp.int32)
        mask = key_pos[None, :] < q_pos[:, None]
        log_beta = jax.nn.log_sigmoid(z)
        log_stay = jnp.where(mask, jax.nn.log_sigmoid(-z), 0.0)
        after = lax.cumsum(log_stay, axis=3, reverse=True) - log_stay
        w = jnp.where(mask, jnp.exp(log_beta + after), 0.0)
        return jnp.einsum('bhqk,bhkd->bhqd', w, vf)

    o = lax.map(block, (qb, starts))
    o = o.transpose(1, 0, 3, 2, 4).reshape(B, S, SB_W)
    return o.astype(q.dtype)


def peer_ffn(x, w_pq, sub_k1, sub_k2, u_emb, v_emb):
    B, S, D = x.shape
    T = B * S
    half = PEER_DKEY // 2
    xt = x.reshape(T, D)
    q = (xt @ w_pq).astype(jnp.float32).reshape(T, PEER_HEADS, PEER_DKEY)
    s1 = jnp.einsum('thd,hnd->thn', q[..., :half], sub_k1.astype(jnp.float32))
    s2 = jnp.einsum('thd,hnd->thn', q[..., half:], sub_k2.astype(jnp.float32))
    v1, i1 = lax.top_k(s1, PEER_TOPK)
    v2, i2 = lax.top_k(s2, PEER_TOPK)
    cand = (v1[..., :, None] + v2[..., None, :]).reshape(T, PEER_HEADS, PEER_TOPK * PEER_TOPK)
    cand_idx = (i1[..., :, None] * PEER_NKEYS + i2[..., None, :]).reshape(T, PEER_HEADS, PEER_TOPK * PEER_TOPK)
    top_s, top_c = lax.top_k(cand, PEER_TOPK)
    expert = jnp.take_along_axis(cand_idx, top_c, axis=-1)
    gate = jax.nn.softmax(top_s, axis=-1)

    NB = T // PEER_TOK_BLOCK
    K = PEER_HEADS * PEER_TOPK
    xb = xt.reshape(NB, PEER_TOK_BLOCK, D)
    eb = expert.reshape(NB, PEER_TOK_BLOCK, K)
    gb = gate.reshape(NB, PEER_TOK_BLOCK, K)

    def block(args):
        xblk, eblk, gblk = args
        u = jnp.take(u_emb, eblk, axis=0)
        h = jax.nn.gelu(jnp.einsum('tkd,td->tk', u, xblk).astype(jnp.float32))
        a = (h * gblk).astype(xblk.dtype)
        vv = jnp.take(v_emb, eblk, axis=0)
        return jnp.einsum('tk,tkd->td', a, vv).astype(xblk.dtype)

    out = lax.map(block, (xb, eb, gb))
    return out.reshape(B, S, D)


def setup_inputs(seed: int = 0) -> dict:
    key = jax.random.key(seed)
    ks = jax.random.split(key, 24)
    n = jax.random.normal
    f = jnp.float32
    L = DEPTH
    return {
        "x": n(ks[0], (BATCH, SEQ, D_MODEL), f),
        "p": n(ks[1], (L, BATCH, SEQ, PLE_DIM), f),
        "g_mix": 1.0 + 0.01 * n(ks[2], (L, D_MODEL), f),
        "w_in": n(ks[3], (L, D_MODEL, IN_WIDTH), f) * D_MODEL ** -0.5,
        "w_gla_a_up": n(ks[4], (L, GLA_GATE_RANK, GLA_QK), f) * GLA_GATE_RANK ** -0.5,
        "b_gla_a": 0.1 * n(ks[5], (L, GLA_QK), f),
        "g_gla_o": 1.0 + 0.01 * n(ks[6], (L, GLA_V), f),
        "w_gla_out": n(ks[7], (L, GLA_V, D_MODEL), f) * GLA_V ** -0.5,
        "w_sb_out": n(ks[8], (L, SB_W, D_MODEL), f) * SB_W ** -0.5,
        "w_o": n(ks[9], (L, D_MODEL, D_MODEL), f) * D_MODEL ** -0.5,
        "g_ffn": 1.0 + 0.01 * n(ks[10], (L, D_MODEL), f),
        "w_peer_q": n(ks[11], (L, D_MODEL, PEER_HEADS * PEER_DKEY), f) * D_MODEL ** -0.5,
        "peer_k1": n(ks[12], (L, PEER_HEADS, PEER_NKEYS, PEER_DKEY // 2), f) * (PEER_DKEY // 2) ** -0.5,
        "peer_k2": n(ks[13], (L, PEER_HEADS, PEER_NKEYS, PEER_DKEY // 2), f) * (PEER_DKEY // 2) ** -0.5,
        "peer_u": n(ks[14], (L, PEER_N, D_MODEL), f) * D_MODEL ** -0.5,
        "peer_v": n(ks[15], (L, PEER_N, D_MODEL), f) * PEER_HEADS ** -0.5,
        "g_ple": 1.0 + 0.01 * n(ks[16], (L, D_MODEL), f),
        "w_ple_gate": n(ks[17], (L, D_MODEL, D_MODEL), f) * D_MODEL ** -0.5,
        "w_ple": n(ks[18], (L, PLE_DIM, D_MODEL), f) * PLE_DIM ** -0.5,
        "g_final": 1.0 + 0.01 * n(ks[19], (D_MODEL,), f),
    }


def reference(x, p, g_mix, w_in, w_gla_a_up, b_gla_a, g_gla_o, w_gla_out, w_sb_out, w_o,
              g_ffn, w_peer_q, peer_k1, peer_k2, peer_u, peer_v, g_ple, w_ple_gate, w_ple, g_final):
    offsets = [int(o) for o in np.cumsum(IN_SPLITS)[:-1]]
    h = x
    for i in range(DEPTH):
        u = rmsnorm(h, g_mix[i])
        proj = u @ w_in[i]
        (gq, gk, gv, gr, ga, sq, sk, sv, gate_a, gate_b) = jnp.split(proj, offsets, axis=-1)
        y_a = gla_branch(gq, gk, gv, gr, ga, w_gla_a_up[i], b_gla_a[i], g_gla_o[i]) @ w_gla_out[i]
        y_b = stick_breaking_branch(sq, sk, sv) @ w_sb_out[i]
        mixed = jax.nn.sigmoid(gate_a) * y_a + jax.nn.sigmoid(gate_b) * y_b
        h = h + (mixed @ w_o[i]).astype(h.dtype)
        h = h + peer_ffn(rmsnorm(h, g_ffn[i]), w_peer_q[i], peer_k1[i], peer_k2[i],
                         peer_u[i], peer_v[i]).astype(h.dtype)
        ple = p[i] @ w_ple[i]
        ple_gate = jax.nn.sigmoid(rmsnorm(h, g_ple[i]) @ w_ple_gate[i])
        h = h + (ple * ple_gate).astype(h.dtype)
    return rmsnorm(h, g_final)
```

```python
import functools
import math

import jax
import jax.numpy as jnp
from jax import lax
from jax.experimental import pallas as pl
from jax.experimental.pallas import tpu as pltpu

F32 = jnp.float32
BF16 = jnp.bfloat16

D_MODEL = 1024
EPS = 1e-6
CHUNK = 64
GLA_HEADS, GLA_DK, GLA_DV, GLA_RANK, GLA_TAU = 4, 128, 256, 16, 16.0
SB_HEADS, SB_DH, SB_BLOCK = 16, 64, 128
PEER_HEADS, PEER_NKEYS, PEER_TOPK = 8, 128, 16
PEER_HALF = 128
PLE_DIM = 256

LANES = 128
VMEM_LIMIT = 56 * 1024 * 1024

SB_EXIT = -64.0

GLA_GROUP = 4 * CHUNK

NEG_INF = float("-inf")


def _params(*sem):
    return pltpu.CompilerParams(dimension_semantics=sem, vmem_limit_bytes=VMEM_LIMIT)


def _rms(x, g):
    return x * lax.rsqrt(jnp.mean(x * x, axis=-1, keepdims=True) + EPS) * g


def _sigmoid(x):
    return 1.0 / (1.0 + jnp.exp(-x))


def _log_sigmoid(x):
    return jnp.minimum(x, 0.0) - jnp.log(1.0 + jnp.exp(-jnp.abs(x)))


def _gelu_tanh(x):
    c = math.sqrt(2.0 / math.pi)
    return 0.5 * x * (1.0 + jnp.tanh(c * (x + 0.044715 * (x * x * x))))


def _dot(a, b):
    return jnp.dot(a, b, preferred_element_type=F32)


def _dot_nt(a, b):
    return lax.dot_general(a, b, (((1,), (1,)), ((), ())), preferred_element_type=F32)


def _dot_tn(a, b):
    return lax.dot_general(a, b, (((0,), (0,)), ((), ())), preferred_element_type=F32)


def _dot_f32(a, b):
    return jnp.dot(a, b, preferred_element_type=F32, precision=lax.Precision.HIGHEST)


def _inproj_kernel(x_ref, g_ref, w_ref, wga_ref, o_ref, ga_ref, u_scr):
    @pl.when(pl.program_id(1) == 0)
    def _():
        u_scr[...] = _rms(x_ref[...], g_ref[...]).astype(BF16)
        ga_ref[...] = _dot(u_scr[...], wga_ref[...])

    o_ref[...] = _dot(u_scr[...], w_ref[...]).astype(BF16)


def _inproj(x2, g, w_main, w_ga):
    T = x2.shape[0]
    N = w_main.shape[1]
    tm = min(1024, T)
    tn = 1024
    return pl.pallas_call(
        _inproj_kernel,
        grid=(T // tm, N // tn),
        in_specs=[
            pl.BlockSpec((tm, D_MODEL), lambda i, j: (i, 0)),
            pl.BlockSpec((1, D_MODEL), lambda i, j: (0, 0)),
            pl.BlockSpec((D_MODEL, tn), lambda i, j: (0, j)),
            pl.BlockSpec((D_MODEL, LANES), lambda i, j: (0, 0)),
        ],
        out_specs=[
            pl.BlockSpec((tm, tn), lambda i, j: (i, j)),
            pl.BlockSpec((tm, LANES), lambda i, j: (i, 0)),
        ],
        out_shape=[
            jax.ShapeDtypeStruct((T, N), BF16),
            jax.ShapeDtypeStruct((T, LANES), F32),
        ],
        scratch_shapes=[pltpu.VMEM((tm, D_MODEL), BF16)],
        compiler_params=_params("parallel", "arbitrary"),
        name="inproj",
    )(x2, g, w_main, w_ga)


def _gla_kernel(q_ref, k_ref, v_ref, r_ref, ga_ref, wup_ref, ba_ref, go_ref, o_ref, st_ref):
    S = q_ref.shape[1]
    G = GLA_GROUP
    row = lax.broadcasted_iota(jnp.int32, (G, G), 0)
    col = lax.broadcasted_iota(jnp.int32, (G, G), 1)
    same = (row // CHUNK) == (col // CHUNK)
    cum_m = jnp.where(same & (col <= row), 1.0, 0.0).astype(F32)
    tot_m = jnp.where(same, 1.0, 0.0).astype(F32)
    st_ref[...] = jnp.zeros_like(st_ref)

    def group(gi, _):
        rows = pl.ds(pl.multiple_of(gi * G, G), G)
        pre = _dot_f32(ga_ref[0, rows, :], wup_ref[...]) + ba_ref[...]
        log_a = _log_sigmoid(pre) * (1.0 / GLA_TAU)
        logb = _dot_f32(cum_m, log_a)
        tot = _dot_f32(tot_m, log_a)
        k_dec = (k_ref[0, rows, :].astype(F32) * jnp.exp(tot - logb)).astype(BF16)
        decay = jnp.exp(tot)
        q = q_ref[0, rows, :]
        v = v_ref[0, rows, :]
        outs = []
        for c in range(G // CHUNK):
            cs = slice(c * CHUNK, (c + 1) * CHUNK)
            kv_t = _dot_tn(v[cs], k_dec[cs])
            st = decay[c * CHUNK:c * CHUNK + 1, :] * st_ref[...] + kv_t
            st_ref[...] = st
            outs.append(_dot_nt(q[cs], st.astype(BF16)))
        o = jnp.concatenate(outs, axis=0) * (GLA_DK ** -0.5)
        o = _rms(o, go_ref[...])
        r = r_ref[0, rows, :].astype(F32)
        o_ref[0, rows, :] = (o * (r * _sigmoid(r))).astype(BF16)
        return 0

    lax.fori_loop(0, S // G, group, 0)


def _gla(proj3, ga3, wup, ba, go):
    B, S, _ = proj3.shape
    H = GLA_HEADS
    kb = (H * GLA_DK) // GLA_DK
    vb = (2 * H * GLA_DK) // GLA_DV
    rb = vb + H
    return pl.pallas_call(
        _gla_kernel,
        grid=(B, H),
        in_specs=[
            pl.BlockSpec((1, S, GLA_DK), lambda b, h: (b, 0, h)),
            pl.BlockSpec((1, S, GLA_DK), lambda b, h: (b, 0, kb + h)),
            pl.BlockSpec((1, S, GLA_DV), lambda b, h: (b, 0, vb + h)),
            pl.BlockSpec((1, S, GLA_DV), lambda b, h: (b, 0, rb + h)),
            pl.BlockSpec((1, S, LANES), lambda b, h: (b, 0, 0)),
            pl.BlockSpec((LANES, GLA_DK), lambda b, h: (0, h)),
            pl.BlockSpec((1, GLA_DK), lambda b, h: (0, h)),
            pl.BlockSpec((1, GLA_DV), lambda b, h: (0, h)),
        ],
        out_specs=pl.BlockSpec((1, S, GLA_DV), lambda b, h: (b, 0, h)),
        out_shape=jax.ShapeDtypeStruct((B, S, H * GLA_DV), BF16),
        scratch_shapes=[pltpu.VMEM((GLA_DV, GLA_DK), F32)],
        compiler_params=_params("parallel", "parallel"),
        name="gla",
    )(proj3, proj3, proj3, proj3, ga3, wup, ba, go)


SB_PAIRS = 2


def _sb_kernel(q_ref, k_ref, v_ref, o_ref, acc_ref, car_ref):
    i = pl.program_id(2)
    Q = SB_BLOCK
    nch = 2 * SB_PAIRS
    row = lax.broadcasted_iota(jnp.int32, (Q, Q), 0)
    col = lax.broadcasted_iota(jnp.int32, (Q, Q), 1)
    lane = lax.broadcasted_iota(jnp.int32, (Q, LANES), 1)
    uo = jnp.concatenate([jnp.where(row > col, 1.0, 0.0), jnp.ones((Q, Q), F32)], axis=1).astype(BF16)
    lower = lane < SB_DH

    qs = []
    for p in range(SB_PAIRS):
        q2 = q_ref[0, :, p * LANES:(p + 1) * LANES] * (SB_DH ** -0.5)
        zero = jnp.zeros_like(q2)
        qs.append(jnp.where(lower, q2, zero))
        qs.append(jnp.where(lower, zero, q2))

    def block(j, diag):
        rows = pl.ds(pl.multiple_of(j * Q, Q), Q)
        worst = None
        for c in range(nch):
            p = c // 2
            kj = k_ref[0, rows, p * LANES:(p + 1) * LANES]
            vj = v_ref[0, rows, p * LANES:(p + 1) * LANES]
            z = _dot_nt(qs[c], kj)
            ls = _log_sigmoid(z)
            stay = ls - z
            if diag:
                mask = col < row
                stay = jnp.where(mask, stay, 0.0)
            hi = stay.astype(BF16)
            lo = (stay - hi.astype(F32)).astype(BF16)
            sums = _dot(hi, uo) + _dot(lo, uo)
            if diag:
                w = jnp.where(mask, jnp.exp(ls + sums[:, :Q]), 0.0)
                acc_ref[c] = _dot(w.astype(BF16), vj)
                car = sums[:, Q:]
            else:
                w = jnp.exp(ls + sums[:, :Q] + car_ref[c])
                acc_ref[c] += _dot(w.astype(BF16), vj)
                car = car_ref[c] + sums[:, Q:]
            car_ref[c] = car
            m = jnp.max(car)
            worst = m if worst is None else jnp.maximum(worst, m)
        return worst

    worst0 = block(i, True)

    def cond(s):
        j, worst = s
        return (j >= 0) & (worst > SB_EXIT)

    def body(s):
        j, _ = s
        return j - 1, block(j, False)

    lax.while_loop(cond, body, (i - 1, worst0))

    for p in range(SB_PAIRS):
        o_ref[0, :, p * LANES:(p + 1) * LANES] = jnp.where(lower, acc_ref[2 * p], acc_ref[2 * p + 1]).astype(BF16)


def _sb(proj3):
    B, S, _ = proj3.shape
    W = SB_PAIRS * LANES
    base = (2 * GLA_HEADS * GLA_DK + 2 * GLA_HEADS * GLA_DV) // W
    nb = (SB_HEADS * SB_DH) // W
    return pl.pallas_call(
        _sb_kernel,
        grid=(B, nb, S // SB_BLOCK),
        in_specs=[
            pl.BlockSpec((1, SB_BLOCK, W), lambda b, p, i: (b, i, base + p)),
            pl.BlockSpec((1, S, W), lambda b, p, i: (b, 0, base + nb + p)),
            pl.BlockSpec((1, S, W), lambda b, p, i: (b, 0, base + 2 * nb + p)),
        ],
        out_specs=pl.BlockSpec((1, SB_BLOCK, W), lambda b, p, i: (b, i, p)),
        out_shape=jax.ShapeDtypeStruct((B, S, SB_HEADS * SB_DH), BF16),
        scratch_shapes=[
            pltpu.VMEM((2 * SB_PAIRS, SB_BLOCK, LANES), F32),
            pltpu.VMEM((2 * SB_PAIRS, SB_BLOCK, SB_BLOCK), F32),
        ],
        compiler_params=_params("parallel", "parallel", "arbitrary"),
        name="sb",
    )(proj3, proj3, proj3)


def _mix_kernel(gla_ref, sb_ref, ga_ref, gb_ref, x_ref, wa_ref, wb_ref, wo_ref, g_ref, h_ref, xt_ref):
    ya = _dot(gla_ref[...], wa_ref[...])
    yb = _dot(sb_ref[...], wb_ref[...])
    mixed = _sigmoid(ga_ref[...].astype(F32)) * ya + _sigmoid(gb_ref[...].astype(F32)) * yb
    h = x_ref[...] + _dot(mixed.astype(BF16), wo_ref[...])
    h_ref[...] = h
    xt_ref[...] = _rms(h, g_ref[...]).T.astype(BF16)


def _mix(gla_o, sb_o, proj, x2, wa, wb, wo, g):
    T = x2.shape[0]
    tm = min(512, T)
    D = D_MODEL
    gate_blk = (proj.shape[1] - 2 * D) // D
    full = lambda i: (0, 0)
    return pl.pallas_call(
        _mix_kernel,
        grid=(T // tm,),
        in_specs=[
            pl.BlockSpec((tm, D), lambda i: (i, 0)),
            pl.BlockSpec((tm, D), lambda i: (i, 0)),
            pl.BlockSpec((tm, D), lambda i: (i, gate_blk)),
            pl.BlockSpec((tm, D), lambda i: (i, gate_blk + 1)),
            pl.BlockSpec((tm, D), lambda i: (i, 0)),
            pl.BlockSpec((D, D), full),
            pl.BlockSpec((D, D), full),
            pl.BlockSpec((D, D), full),
            pl.BlockSpec((1, D), full),
        ],
        out_specs=[
            pl.BlockSpec((tm, D), lambda i: (i, 0)),
            pl.BlockSpec((D, tm), lambda i: (0, i)),
        ],
        out_shape=[
            jax.ShapeDtypeStruct((T, D), F32),
            jax.ShapeDtypeStruct((D, T), BF16),
        ],
        compiler_params=_params("parallel"),
        name="mix",
    )(gla_o, sb_o, proj, proj, x2, wa, wb, wo, g)


def _top_rows(s, k):
    t = s.shape[1]
    rid = lax.broadcasted_iota(jnp.int32, (k, t), 0)
    out = jnp.full((k, t), NEG_INF, F32)
    for r in range(k):
        m = jnp.max(s, axis=0, keepdims=True)
        out = jnp.where(rid == r, m, out)
        if r + 1 < k:
            s = jnp.where(s == m, NEG_INF, s)
    return out


def _route_kernel(xt_ref, wq_ref, k1_ref, k2_ref, s1_ref, e1_ref, s2_ref, e2_ref, thr_ref):
    K = PEER_TOPK
    tt = xt_ref.shape[1]
    rid8 = lax.broadcasted_iota(jnp.int32, (8, tt), 0)
    xt = xt_ref[...]
    for h in range(PEER_HEADS):
        qt = _dot(wq_ref[h * 2 * PEER_HALF:(h + 1) * 2 * PEER_HALF, :], xt).astype(BF16)
        s1 = _dot(k1_ref[h], qt[:PEER_HALF])
        s2 = _dot(k2_ref[h], qt[PEER_HALF:])
        v1 = _top_rows(s1, K)
        v2 = _top_rows(s2, K)
        slabs = []
        for a in range(K):
            nb = K // (a + 1)
            if nb > 8:
                slabs.append(v1[a:a + 1] + v2)
            else:
                slabs.append(jnp.where(rid8 < nb, v1[a:a + 1] + v2[:8], NEG_INF))
        top = _top_rows(jnp.concatenate(slabs, axis=0), K)
        z = jnp.sum(jnp.exp(top - top[0:1]), axis=0, keepdims=True)
        s1_ref[:, h, :] = s1
        s2_ref[h] = s2
        e1_ref[:, h, :] = jnp.exp(s1 - v1[0:1]) / z
        e2_ref[h] = jnp.exp(s2 - v2[0:1])
        thr_ref[h:h + 1, :] = top[K - 1:K]


def _route(xt, wq_t, k1, k2):
    D, T = xt.shape
    H, N = PEER_HEADS, PEER_NKEYS
    tt = min(512, T)
    return pl.pallas_call(
        _route_kernel,
        grid=(T // tt,),
        in_specs=[
            pl.BlockSpec((D, tt), lambda i: (0, i)),
            pl.BlockSpec(wq_t.shape, lambda i: (0, 0)),
            pl.BlockSpec(k1.shape, lambda i: (0, 0, 0)),
            pl.BlockSpec(k2.shape, lambda i: (0, 0, 0)),
        ],
        out_specs=[
            pl.BlockSpec((N, H, tt), lambda i: (0, 0, i)),
            pl.BlockSpec((N, H, tt), lambda i: (0, 0, i)),
            pl.BlockSpec((H, N, tt), lambda i: (0, 0, i)),
            pl.BlockSpec((H, N, tt), lambda i: (0, 0, i)),
            pl.BlockSpec((H, tt), lambda i: (0, i)),
        ],
        out_shape=[
            jax.ShapeDtypeStruct((N, H, T), F32),
            jax.ShapeDtypeStruct((N, H, T), F32),
            jax.ShapeDtypeStruct((H, N, T), F32),
            jax.ShapeDtypeStruct((H, N, T), F32),
            jax.ShapeDtypeStruct((H, T), F32),
        ],
        compiler_params=_params("parallel"),
        name="route",
    )(xt, wq_t, k1, k2)


PEER_NI = 4


def _peer_kernel(xt_ref, u_ref, vt_ref, s1_ref, e1_ref, s2_ref, e2_ref, thr_ref, o_ref, a_scr):
    j = pl.program_id(1)
    tt = xt_ref.shape[1]
    N = PEER_NKEYS

    @pl.when(j == 0)
    def _():
        o_ref[...] = jnp.zeros_like(o_ref)

    act = _gelu_tanh(_dot(u_ref[...], xt_ref[...]))
    for n in range(PEER_NI):
        i1 = j * PEER_NI + n
        s1 = s1_ref[i1]
        e1 = e1_ref[i1]
        for tc in range(tt // LANES):
            ts = slice(tc * LANES, (tc + 1) * LANES)
            g = jnp.zeros((N, LANES), F32)
            for h in range(PEER_HEADS):
                sel = (s2_ref[h, :, ts] + s1[h:h + 1, ts]) >= thr_ref[h:h + 1, ts]
                g = g + jnp.where(sel, e2_ref[h, :, ts], 0.0) * e1[h:h + 1, ts]
            a_scr[n * N:(n + 1) * N, ts] = (act[n * N:(n + 1) * N, ts] * g).astype(BF16)
    o_ref[...] += _dot(vt_ref[...], a_scr[...])


def _peer(xt, u_bf, vt_bf, s1, e1, s2, e2, thr):
    D, T = xt.shape
    H, N = PEER_HEADS, PEER_NKEYS
    tt = min(512, T)
    te = PEER_NI * N
    return pl.pallas_call(
        _peer_kernel,
        grid=(T // tt, (N * N) // te),
        in_specs=[
            pl.BlockSpec((D, tt), lambda i, j: (0, i)),
            pl.BlockSpec((te, D), lambda i, j: (j, 0)),
            pl.BlockSpec((D, te), lambda i, j: (0, j)),
            pl.BlockSpec((N, H, tt), lambda i, j: (0, 0, i)),
            pl.BlockSpec((N, H, tt), lambda i, j: (0, 0, i)),
            pl.BlockSpec((H, N, tt), lambda i, j: (0, 0, i)),
            pl.BlockSpec((H, N, tt), lambda i, j: (0, 0, i)),
            pl.BlockSpec((H, tt), lambda i, j: (0, i)),
        ],
        out_specs=pl.BlockSpec((D, tt), lambda i, j: (0, i)),
        out_shape=jax.ShapeDtypeStruct((D, T), F32),
        scratch_shapes=[pltpu.VMEM((te, tt), BF16)],
        compiler_params=_params("parallel", "arbitrary"),
        name="peer",
    )(xt, u_bf, vt_bf, s1, e1, s2, e2, thr)


def _final_kernel(h_ref, pt_ref, p_ref, wp_ref, wg_ref, gp_ref, gf_ref, o_ref):
    h2 = h_ref[...] + pt_ref[...].T
    ple = _dot(p_ref[...].astype(BF16), wp_ref[...])
    gate = _sigmoid(_dot(_rms(h2, gp_ref[...]).astype(BF16), wg_ref[...]))
    o_ref[...] = _rms(h2 + ple * gate, gf_ref[...])


def _final(h1, peer_t, p2, wp, wg, gp, gf):
    T, D = h1.shape
    tm = min(512, T)
    full = lambda i: (0, 0)
    return pl.pallas_call(
        _final_kernel,
        grid=(T // tm,),
        in_specs=[
            pl.BlockSpec((tm, D), lambda i: (i, 0)),
            pl.BlockSpec((D, tm), lambda i: (0, i)),
            pl.BlockSpec((tm, PLE_DIM), lambda i: (i, 0)),
            pl.BlockSpec((PLE_DIM, D), full),
            pl.BlockSpec((D, D), full),
            pl.BlockSpec((1, D), full),
            pl.BlockSpec((1, D), full),
        ],
        out_specs=pl.BlockSpec((tm, D), lambda i: (i, 0)),
        out_shape=jax.ShapeDtypeStruct((T, D), F32),
        compiler_params=_params("parallel"),
        name="final",
    )(h1, peer_t, p2, wp, wg, gp, gf)


def kernel(x, p, g_mix, w_in, w_gla_a_up, b_gla_a, g_gla_o, w_gla_out, w_sb_out, w_o, g_ffn, w_peer_q, peer_k1, peer_k2, peer_u, peer_v, g_ple, w_ple_gate, w_ple, g_final):
    B, S, D = x.shape
    T = B * S
    depth = w_in.shape[0]
    assert depth == 1, "the final RMSNorm is fused into the last stage of a single layer"
    qk = GLA_HEADS * GLA_DK
    gv = GLA_HEADS * GLA_DV
    lo = 2 * qk + 2 * gv
    h = x.reshape(T, D)
    for i in range(depth):
        w_main = jnp.concatenate([w_in[i][:, :lo], w_in[i][:, lo + GLA_RANK:]], axis=1).astype(BF16)
        w_ga = jnp.pad(w_in[i][:, lo:lo + GLA_RANK], ((0, 0), (0, LANES - GLA_RANK))).astype(BF16)
        wup = jnp.pad(w_gla_a_up[i], ((0, LANES - GLA_RANK), (0, 0)))

        proj, a_lo = _inproj(h, g_mix[i][None], w_main, w_ga)
        proj3 = proj.reshape(B, S, -1)
        gla_o = _gla(proj3, a_lo.reshape(B, S, LANES), wup, b_gla_a[i][None], g_gla_o[i][None])
        sb_o = _sb(proj3)
        h1, xt = _mix(gla_o.reshape(T, -1), sb_o.reshape(T, -1), proj, h,
                      w_gla_out[i].astype(BF16), w_sb_out[i].astype(BF16), w_o[i].astype(BF16), g_ffn[i][None])
        s1, e1, s2, e2, thr = _route(xt, w_peer_q[i].T.astype(BF16), peer_k1[i].astype(BF16), peer_k2[i].astype(BF16))
        peer_t = _peer(xt, peer_u[i].astype(BF16), peer_v[i].T.astype(BF16), s1, e1, s2, e2, thr)
        h = _final(h1, peer_t, p[i].reshape(T, -1), w_ple[i].astype(BF16), w_ple_gate[i].astype(BF16),
                   g_ple[i][None], g_final[None])
    return h.reshape(B, S, D)
```

```python
import functools
import math

import jax
import jax.numpy as jnp
from jax import lax
from jax.experimental import pallas as pl
from jax.experimental.pallas import tpu as pltpu

F32 = jnp.float32
BF16 = jnp.bfloat16

D_MODEL = 1024
EPS = 1e-6
CHUNK = 64
GLA_HEADS, GLA_DK, GLA_DV, GLA_RANK, GLA_TAU = 4, 128, 256, 16, 16.0
SB_HEADS, SB_DH, SB_BLOCK = 16, 64, 128
PEER_HEADS, PEER_NKEYS, PEER_TOPK = 8, 128, 16
PEER_HALF = 128
PLE_DIM = 256

LANES = 128
VMEM_LIMIT = 56 * 1024 * 1024

SB_EXIT = -64.0

GLA_GROUP = 4 * CHUNK

NEG_INF = float("-inf")


def _params(*sem):
    return pltpu.CompilerParams(dimension_semantics=sem, vmem_limit_bytes=VMEM_LIMIT)


def _rms(x, g):
    return x * lax.rsqrt(jnp.mean(x * x, axis=-1, keepdims=True) + EPS) * g


def _sigmoid(x):
    return 1.0 / (1.0 + jnp.exp(-x))


def _log_sigmoid(x):
    return jnp.minimum(x, 0.0) - jnp.log(1.0 + jnp.exp(-jnp.abs(x)))


def _gelu_tanh(x):
    c = math.sqrt(2.0 / math.pi)
    return 0.5 * x * (1.0 + jnp.tanh(c * (x + 0.044715 * (x * x * x))))


def _gelu_bf16(x):
    k = -2.0 * math.sqrt(2.0 / math.pi) * math.log2(math.e)
    t = x * ((k * 0.044715) * (x * x) + k)
    return x * (1.0 / (1.0 + jnp.exp2(t)))


def _dot(a, b):
    return jnp.dot(a, b, preferred_element_type=F32)


def _dot_nt(a, b):
    return lax.dot_general(a, b, (((1,), (1,)), ((), ())), preferred_element_type=F32)


def _dot_tn(a, b):
    return lax.dot_general(a, b, (((0,), (0,)), ((), ())), preferred_element_type=F32)


def _dot_f32(a, b):
    return jnp.dot(a, b, preferred_element_type=F32, precision=lax.Precision.HIGHEST)


def _inproj_kernel(x_ref, g_ref, w_ref, wga_ref, o_ref, ga_ref, u_scr):
    @pl.when(pl.program_id(1) == 0)
    def _():
        u_scr[...] = _rms(x_ref[...], g_ref[...]).astype(BF16)
        ga_ref[...] = _dot(u_scr[...], wga_ref[...])

    o_ref[...] = _dot(u_scr[...], w_ref[...]).astype(BF16)


def _inproj(x2, g, w_main, w_ga):
    T = x2.shape[0]
    N = w_main.shape[1]
    tm = min(1024, T)
    tn = 1024
    return pl.pallas_call(
        _inproj_kernel,
        grid=(T // tm, N // tn),
        in_specs=[
            pl.BlockSpec((tm, D_MODEL), lambda i, j: (i, 0)),
            pl.BlockSpec((1, D_MODEL), lambda i, j: (0, 0)),
            pl.BlockSpec((D_MODEL, tn), lambda i, j: (0, j)),
            pl.BlockSpec((D_MODEL, LANES), lambda i, j: (0, 0)),
        ],
        out_specs=[
            pl.BlockSpec((tm, tn), lambda i, j: (i, j)),
            pl.BlockSpec((tm, LANES), lambda i, j: (i, 0)),
        ],
        out_shape=[
            jax.ShapeDtypeStruct((T, N), BF16),
            jax.ShapeDtypeStruct((T, LANES), F32),
        ],
        scratch_shapes=[pltpu.VMEM((tm, D_MODEL), BF16)],
        compiler_params=_params("parallel", "arbitrary"),
        name="inproj",
    )(x2, g, w_main, w_ga)


def _gla_kernel(q_ref, k_ref, v_ref, r_ref, ga_ref, wup_ref, ba_ref, go_ref, o_ref, st_ref):
    S = q_ref.shape[1]
    G = GLA_GROUP
    row = lax.broadcasted_iota(jnp.int32, (G, G), 0)
    col = lax.broadcasted_iota(jnp.int32, (G, G), 1)
    same = (row // CHUNK) == (col // CHUNK)
    cum_m = jnp.where(same & (col <= row), 1.0, 0.0).astype(F32)
    tot_m = jnp.where(same, 1.0, 0.0).astype(F32)
    st_ref[...] = jnp.zeros_like(st_ref)

    def group(gi, _):
        rows = pl.ds(pl.multiple_of(gi * G, G), G)
        pre = _dot_f32(ga_ref[0, rows, :], wup_ref[...]) + ba_ref[...]
        log_a = _log_sigmoid(pre) * (1.0 / GLA_TAU)
        logb = _dot_f32(cum_m, log_a)
        tot = _dot_f32(tot_m, log_a)
        k_dec = (k_ref[0, rows, :].astype(F32) * jnp.exp(tot - logb)).astype(BF16)
        decay = jnp.exp(tot)
        q = q_ref[0, rows, :]
        v = v_ref[0, rows, :]
        outs = []
        for c in range(G // CHUNK):
            cs = slice(c * CHUNK, (c + 1) * CHUNK)
            kv_t = _dot_tn(v[cs], k_dec[cs])
            st = decay[c * CHUNK:c * CHUNK + 1, :] * st_ref[...] + kv_t
            st_ref[...] = st
            outs.append(_dot_nt(q[cs], st.astype(BF16)))
        o = jnp.concatenate(outs, axis=0) * (GLA_DK ** -0.5)
        o = _rms(o, go_ref[...])
        r = r_ref[0, rows, :].astype(F32)
        o_ref[0, rows, :] = (o * (r * _sigmoid(r))).astype(BF16)
        return 0

    lax.fori_loop(0, S // G, group, 0)


def _gla(proj3, ga3, wup, ba, go):
    B, S, _ = proj3.shape
    H = GLA_HEADS
    kb = (H * GLA_DK) // GLA_DK
    vb = (2 * H * GLA_DK) // GLA_DV
    rb = vb + H
    return pl.pallas_call(
        _gla_kernel,
        grid=(B, H),
        in_specs=[
            pl.BlockSpec((1, S, GLA_DK), lambda b, h: (b, 0, h)),
            pl.BlockSpec((1, S, GLA_DK), lambda b, h: (b, 0, kb + h)),
            pl.BlockSpec((1, S, GLA_DV), lambda b, h: (b, 0, vb + h)),
            pl.BlockSpec((1, S, GLA_DV), lambda b, h: (b, 0, rb + h)),
            pl.BlockSpec((1, S, LANES), lambda b, h: (b, 0, 0)),
            pl.BlockSpec((LANES, GLA_DK), lambda b, h: (0, h)),
            pl.BlockSpec((1, GLA_DK), lambda b, h: (0, h)),
            pl.BlockSpec((1, GLA_DV), lambda b, h: (0, h)),
        ],
        out_specs=pl.BlockSpec((1, S, GLA_DV), lambda b, h: (b, 0, h)),
        out_shape=jax.ShapeDtypeStruct((B, S, H * GLA_DV), BF16),
        scratch_shapes=[pltpu.VMEM((GLA_DV, GLA_DK), F32)],
        compiler_params=_params("parallel", "parallel"),
        name="gla",
    )(proj3, proj3, proj3, proj3, ga3, wup, ba, go)


SB_PAIRS = 4


def _sb_kernel(q_ref, k_ref, v_ref, o_ref, acc_ref, car_ref):
    i = pl.program_id(2)
    Q = SB_BLOCK
    nch = 2 * SB_PAIRS
    row = lax.broadcasted_iota(jnp.int32, (Q, Q), 0)
    col = lax.broadcasted_iota(jnp.int32, (Q, Q), 1)
    lane = lax.broadcasted_iota(jnp.int32, (Q, LANES), 1)
    uo = jnp.concatenate([jnp.where(row > col, 1.0, 0.0), jnp.ones((Q, Q), F32)], axis=1).astype(BF16)
    uo2 = jnp.concatenate([uo, uo], axis=0)
    lower = lane < SB_DH
    srow = lax.broadcasted_iota(jnp.int32, (nch * Q, Q), 0)
    scol = lax.broadcasted_iota(jnp.int32, (nch * Q, Q), 1)

    qs = []
    for p in range(SB_PAIRS):
        q2 = q_ref[0, :, p * LANES:(p + 1) * LANES] * (SB_DH ** -0.5)
        zero = jnp.zeros_like(q2)
        qs.append(jnp.where(lower, q2, zero))
        qs.append(jnp.where(lower, zero, q2))

    def block(j, diag):
        rows = pl.ds(pl.multiple_of(j * Q, Q), Q)
        kj = k_ref[0, rows, :]
        vj = v_ref[0, rows, :]
        z = jnp.concatenate([_dot_nt(qs[c], kj[:, (c // 2) * LANES:(c // 2 + 1) * LANES])
                             for c in range(nch)], axis=0)
        ls = _log_sigmoid(z)
        stay = ls - z
        if diag:
            mask = scol < (srow & (Q - 1))
            stay = jnp.where(mask, stay, 0.0)
        hi = stay.astype(BF16)
        lo = (stay - hi.astype(F32)).astype(BF16)
        sums = _dot(jnp.concatenate([hi, lo], axis=1), uo2)
        if diag:
            w = jnp.where(mask, jnp.exp(ls + sums[:, :Q]), 0.0)
            car = sums[:, Q:]
        else:
            w = jnp.exp(ls + sums[:, :Q] + car_ref[...])
            car = car_ref[...] + sums[:, Q:]
        car_ref[...] = car
        wb = w.astype(BF16)
        pv = jnp.concatenate([_dot(wb[c * Q:(c + 1) * Q], vj[:, (c // 2) * LANES:(c // 2 + 1) * LANES])
                              for c in range(nch)], axis=0)
        if diag:
            acc_ref[...] = pv
        else:
            acc_ref[...] += pv
        return jnp.max(car)

    worst0 = block(i, True)

    def cond(s):
        j, worst = s
        return (j >= 0) & (worst > SB_EXIT)

    def body(s):
        j, _ = s
        return j - 1, block(j, False)

    lax.while_loop(cond, body, (i - 1, worst0))

    for p in range(SB_PAIRS):
        even = acc_ref[2 * p * Q:(2 * p + 1) * Q, :]
        odd = acc_ref[(2 * p + 1) * Q:(2 * p + 2) * Q, :]
        o_ref[0, :, p * LANES:(p + 1) * LANES] = jnp.where(lower, even, odd).astype(BF16)


def _sb(proj3):
    B, S, _ = proj3.shape
    W = SB_PAIRS * LANES
    base = (2 * GLA_HEADS * GLA_DK + 2 * GLA_HEADS * GLA_DV) // W
    nb = (SB_HEADS * SB_DH) // W
    return pl.pallas_call(
        _sb_kernel,
        grid=(B, nb, S // SB_BLOCK),
        in_specs=[
            pl.BlockSpec((1, SB_BLOCK, W), lambda b, p, i: (b, i, base + p)),
            pl.BlockSpec((1, S, W), lambda b, p, i: (b, 0, base + nb + p)),
            pl.BlockSpec((1, S, W), lambda b, p, i: (b, 0, base + 2 * nb + p)),
        ],
        out_specs=pl.BlockSpec((1, SB_BLOCK, W), lambda b, p, i: (b, i, p)),
        out_shape=jax.ShapeDtypeStruct((B, S, SB_HEADS * SB_DH), BF16),
        scratch_shapes=[
            pltpu.VMEM((2 * SB_PAIRS * SB_BLOCK, LANES), F32),
            pltpu.VMEM((2 * SB_PAIRS * SB_BLOCK, SB_BLOCK), F32),
        ],
        compiler_params=_params("parallel", "parallel", "arbitrary"),
        name="sb",
    )(proj3, proj3, proj3)


def _mix_kernel(gla_ref, sb_ref, ga_ref, gb_ref, x_ref, wa_ref, wb_ref, wo_ref, g_ref, h_ref, xt_ref):
    ya = _dot(gla_ref[...], wa_ref[...])
    yb = _dot(sb_ref[...], wb_ref[...])
    mixed = _sigmoid(ga_ref[...].astype(F32)) * ya + _sigmoid(gb_ref[...].astype(F32)) * yb
    h = x_ref[...] + _dot(mixed.astype(BF16), wo_ref[...])
    h_ref[...] = h
    xt_ref[...] = _rms(h, g_ref[...]).T.astype(BF16)


def _mix(gla_o, sb_o, proj, x2, wa, wb, wo, g):
    T = x2.shape[0]
    tm = min(512, T)
    D = D_MODEL
    gate_blk = (proj.shape[1] - 2 * D) // D
    full = lambda i: (0, 0)
    return pl.pallas_call(
        _mix_kernel,
        grid=(T // tm,),
        in_specs=[
            pl.BlockSpec((tm, D), lambda i: (i, 0)),
            pl.BlockSpec((tm, D), lambda i: (i, 0)),
            pl.BlockSpec((tm, D), lambda i: (i, gate_blk)),
            pl.BlockSpec((tm, D), lambda i: (i, gate_blk + 1)),
            pl.BlockSpec((tm, D), lambda i: (i, 0)),
            pl.BlockSpec((D, D), full),
            pl.BlockSpec((D, D), full),
            pl.BlockSpec((D, D), full),
            pl.BlockSpec((1, D), full),
        ],
        out_specs=[
            pl.BlockSpec((tm, D), lambda i: (i, 0)),
            pl.BlockSpec((D, tm), lambda i: (0, i)),
        ],
        out_shape=[
            jax.ShapeDtypeStruct((T, D), F32),
            jax.ShapeDtypeStruct((D, T), BF16),
        ],
        compiler_params=_params("parallel"),
        name="mix",
    )(gla_o, sb_o, proj, proj, x2, wa, wb, wo, g)


def _top_rows(s, k):
    t = s.shape[1]
    rid = lax.broadcasted_iota(jnp.int32, (k, t), 0)
    out = jnp.full((k, t), NEG_INF, F32)
    for r in range(k):
        m = jnp.max(s, axis=0, keepdims=True)
        out = jnp.where(rid == r, m, out)
        if r + 1 < k:
            s = jnp.where(s == m, NEG_INF, s)
    return out


def _route_kernel(xt_ref, wq_ref, k1_ref, k2_ref, cnt_ref, e1_ref, rk_ref, e2_ref):
    K = PEER_TOPK
    tt = xt_ref.shape[1]
    rid8 = lax.broadcasted_iota(jnp.int32, (8, tt), 0)
    xt = xt_ref[...]
    for h in range(PEER_HEADS):
        qt = _dot(wq_ref[h * 2 * PEER_HALF:(h + 1) * 2 * PEER_HALF, :], xt).astype(BF16)
        s1 = _dot(k1_ref[h], qt[:PEER_HALF])
        s2 = _dot(k2_ref[h], qt[PEER_HALF:])
        v1 = _top_rows(s1, K)
        v2 = _top_rows(s2, K)
        slabs = []
        for a in range(K):
            nb = K // (a + 1)
            if nb > 8:
                slabs.append(v1[a:a + 1] + v2)
            else:
                slabs.append(jnp.where(rid8 < nb, v1[a:a + 1] + v2[:8], NEG_INF))
        top = _top_rows(jnp.concatenate(slabs, axis=0), K)
        z = jnp.sum(jnp.exp(top - top[0:1]), axis=0, keepdims=True)
        thr = top[K - 1:K]
        cnt = jnp.zeros_like(s1)
        rk = jnp.zeros_like(s2)
        for b in range(K):
            rk = jnp.where(v2[b:b + 1] > s2, b + 1.0, rk)
            pairs = jnp.sum(jnp.where(v1[b:b + 1] + v2 >= thr, 1.0, 0.0), axis=0, keepdims=True)
            cnt = jnp.where(s1 == v1[b:b + 1], pairs, cnt)
        cnt_ref[:, h, :] = cnt
        rk_ref[h] = rk.astype(BF16)
        e1_ref[:, h, :] = jnp.exp(s1 - v1[0:1]) / z
        e2_ref[h] = jnp.exp(s2 - v2[0:1]).astype(BF16)


def _route(xt, wq_t, k1, k2):
    D, T = xt.shape
    H, N = PEER_HEADS, PEER_NKEYS
    tt = min(512, T)
    return pl.pallas_call(
        _route_kernel,
        grid=(T // tt,),
        in_specs=[
            pl.BlockSpec((D, tt), lambda i: (0, i)),
            pl.BlockSpec(wq_t.shape, lambda i: (0, 0)),
            pl.BlockSpec(k1.shape, lambda i: (0, 0, 0)),
            pl.BlockSpec(k2.shape, lambda i: (0, 0, 0)),
        ],
        out_specs=[
            pl.BlockSpec((N, H, tt), lambda i: (0, 0, i)),
            pl.BlockSpec((N, H, tt), lambda i: (0, 0, i)),
            pl.BlockSpec((H, N, tt), lambda i: (0, 0, i)),
            pl.BlockSpec((H, N, tt), lambda i: (0, 0, i)),
        ],
        out_shape=[
            jax.ShapeDtypeStruct((N, H, T), F32),
            jax.ShapeDtypeStruct((N, H, T), F32),
            jax.ShapeDtypeStruct((H, N, T), BF16),
            jax.ShapeDtypeStruct((H, N, T), BF16),
        ],
        compiler_params=_params("parallel"),
        name="route",
    )(xt, wq_t, k1, k2)


PEER_NI = 4
PEER_TT = 512


def _peer_kernel(xt_ref, u_ref, vt_ref, cnt_ref, e1_ref, rk_ref, e2_ref, o_ref, act0, act1, a0, a1):
    s = pl.program_id(1)
    nblk = pl.num_programs(1) - 1
    N = PEER_NKEYS
    te = PEER_NI * N

    @pl.when(s == 0)
    def _():
        o_ref[...] = jnp.zeros_like(o_ref)
        act1[...] = jnp.zeros_like(act1)
        a0[...] = jnp.zeros_like(a0)

    def scores(half, part, act_ref):
        rows = slice(part * (te // 2), (part + 1) * (te // 2))
        h = _dot(u_ref[half * te:(half + 1) * te, :][rows], xt_ref[...]).astype(BF16)
        act_ref[rows, :] = _gelu_bf16(h)

    def gates(blk, half, n, act_ref, a_ref):
        i1 = (jnp.clip(blk, 0, nblk - 1) * 2 + half) * PEER_NI + n
        cnt = cnt_ref[i1]
        e1 = e1_ref[i1]
        g = jnp.zeros((N, cnt.shape[1]), BF16)
        for h in range(PEER_HEADS):
            cb = jnp.broadcast_to(cnt[h:h + 1], g.shape).astype(BF16)
            eb = jnp.broadcast_to(e1[h:h + 1], g.shape).astype(BF16)
            g = g + jnp.where(rk_ref[h] < cb, e2_ref[h], jnp.zeros((), BF16)) * eb
        a_ref[n * N:(n + 1) * N, :] = act_ref[n * N:(n + 1) * N, :] * g

    def mix(half, a_ref):
        o_ref[...] += _dot(vt_ref[:, half * te:(half + 1) * te], a_ref[...])

    scores(0, 0, act0)
    gates(s - 1, 1, 0, act1, a1)
    gates(s - 1, 1, 1, act1, a1)
    mix(0, a0)
    gates(s - 1, 1, 2, act1, a1)
    scores(0, 1, act0)
    gates(s - 1, 1, 3, act1, a1)

    scores(1, 0, act1)
    gates(s, 0, 0, act0, a0)
    gates(s, 0, 1, act0, a0)
    mix(1, a1)
    gates(s, 0, 2, act0, a0)
    scores(1, 1, act1)
    gates(s, 0, 3, act0, a0)


def _peer(xt, u_bf, vt_bf, cnt, e1, rk, e2):
    D, T = xt.shape
    H, N = PEER_HEADS, PEER_NKEYS
    tt = PEER_TT
    te = PEER_NI * N
    nblk = (N * N) // (2 * te)
    return pl.pallas_call(
        _peer_kernel,
        grid=(T // tt, nblk + 1),
        in_specs=[
            pl.BlockSpec((D, tt), lambda i, s: (0, i)),
            pl.BlockSpec((2 * te, D), lambda i, s: (jnp.minimum(s, nblk - 1), 0)),
            pl.BlockSpec((D, 2 * te), lambda i, s: (0, jnp.maximum(s - 1, 0))),
            pl.BlockSpec((N, H, tt), lambda i, s: (0, 0, i)),
            pl.BlockSpec((N, H, tt), lambda i, s: (0, 0, i)),
            pl.BlockSpec((H, N, tt), lambda i, s: (0, 0, i)),
            pl.BlockSpec((H, N, tt), lambda i, s: (0, 0, i)),
        ],
        out_specs=pl.BlockSpec((D, tt), lambda i, s: (0, i)),
        out_shape=jax.ShapeDtypeStruct((D, T), F32),
        scratch_shapes=[pltpu.VMEM((te, tt), BF16)] * 4,
        compiler_params=_params("parallel", "arbitrary"),
        name="peer",
    )(xt, u_bf, vt_bf, cnt, e1, rk, e2)


def _final_kernel(h_ref, pt_ref, p_ref, wp_ref, wg_ref, gp_ref, gf_ref, o_ref):
    h2 = h_ref[...] + pt_ref[...].T
    ple = _dot(p_ref[...].astype(BF16), wp_ref[...])
    gate = _sigmoid(_dot(_rms(h2, gp_ref[...]).astype(BF16), wg_ref[...]))
    o_ref[...] = _rms(h2 + ple * gate, gf_ref[...])


def _final(h1, peer_t, p2, wp, wg, gp, gf):
    T, D = h1.shape
    tm = min(512, T)
    full = lambda i: (0, 0)
    return pl.pallas_call(
        _final_kernel,
        grid=(T // tm,),
        in_specs=[
            pl.BlockSpec((tm, D), lambda i: (i, 0)),
            pl.BlockSpec((D, tm), lambda i: (0, i)),
            pl.BlockSpec((tm, PLE_DIM), lambda i: (i, 0)),
            pl.BlockSpec((PLE_DIM, D), full),
            pl.BlockSpec((D, D), full),
            pl.BlockSpec((1, D), full),
            pl.BlockSpec((1, D), full),
        ],
        out_specs=pl.BlockSpec((tm, D), lambda i: (i, 0)),
        out_shape=jax.ShapeDtypeStruct((T, D), F32),
        compiler_params=_params("parallel"),
        name="final",
    )(h1, peer_t, p2, wp, wg, gp, gf)


def kernel(x, p, g_mix, w_in, w_gla_a_up, b_gla_a, g_gla_o, w_gla_out, w_sb_out, w_o, g_ffn, w_peer_q, peer_k1, peer_k2, peer_u, peer_v, g_ple, w_ple_gate, w_ple, g_final):
    B, S, D = x.shape
    T = B * S
    depth = w_in.shape[0]
    assert depth == 1, "the final RMSNorm is fused into the last stage of a single layer"
    qk = GLA_HEADS * GLA_DK
    gv = GLA_HEADS * GLA_DV
    lo = 2 * qk + 2 * gv
    h = x.reshape(T, D)
    for i in range(depth):
        w_main = jnp.concatenate([w_in[i][:, :lo], w_in[i][:, lo + GLA_RANK:]], axis=1).astype(BF16)
        w_ga = jnp.pad(w_in[i][:, lo:lo + GLA_RANK], ((0, 0), (0, LANES - GLA_RANK))).astype(BF16)
        wup = jnp.pad(w_gla_a_up[i], ((0, LANES - GLA_RANK), (0, 0)))

        proj, a_lo = _inproj(h, g_mix[i][None], w_main, w_ga)
        proj3 = proj.reshape(B, S, -1)
        gla_o = _gla(proj3, a_lo.reshape(B, S, LANES), wup, b_gla_a[i][None], g_gla_o[i][None])
        sb_o = _sb(proj3)
        h1, xt = _mix(gla_o.reshape(T, -1), sb_o.reshape(T, -1), proj, h,
                      w_gla_out[i].astype(BF16), w_sb_out[i].astype(BF16), w_o[i].astype(BF16), g_ffn[i][None])
        cnt, e1, rk, e2 = _route(xt, w_peer_q[i].T.astype(BF16), peer_k1[i].astype(BF16), peer_k2[i].astype(BF16))
        peer_t = _peer(xt, peer_u[i].astype(BF16), peer_v[i].T.astype(BF16), cnt, e1, rk, e2)
        h = _final(h1, peer_t, p[i].reshape(T, -1), w_ple[i].astype(BF16), w_ple_gate[i].astype(BF16),
                   g_ple[i][None], g_final[None])
    return h.reshape(B, S, D)
```

```python
import functools
import math

import jax
import jax.numpy as jnp
from jax import lax
from jax.experimental import pallas as pl
from jax.experimental.pallas import tpu as pltpu

F32 = jnp.float32
BF16 = jnp.bfloat16

D_MODEL = 1024
EPS = 1e-6
CHUNK = 64
GLA_HEADS, GLA_DK, GLA_DV, GLA_RANK, GLA_TAU = 4, 128, 256, 16, 16.0
SB_HEADS, SB_DH, SB_BLOCK = 16, 64, 128
PEER_HEADS, PEER_NKEYS, PEER_TOPK = 8, 128, 16
PEER_HALF = 128
PLE_DIM = 256

LANES = 128
VMEM_LIMIT = 56 * 1024 * 1024

SB_EXIT = -64.0

GLA_GROUP = 4 * CHUNK

NEG_INF = float("-inf")


def _params(*sem):
    return pltpu.CompilerParams(dimension_semantics=sem, vmem_limit_bytes=VMEM_LIMIT)


def _rms(x, g):
    return x * lax.rsqrt(jnp.mean(x * x, axis=-1, keepdims=True) + EPS) * g


def _sigmoid(x):
    return 1.0 / (1.0 + jnp.exp(-x))


def _log_sigmoid(x):
    return jnp.minimum(x, 0.0) - jnp.log(1.0 + jnp.exp(-jnp.abs(x)))


def _gelu_tanh(x):
    c = math.sqrt(2.0 / math.pi)
    return 0.5 * x * (1.0 + jnp.tanh(c * (x + 0.044715 * (x * x * x))))


def _gelu_consts():
    k = -2.0 * math.sqrt(2.0 / math.pi) * math.log2(math.e)
    return jnp.zeros((8, LANES), F32).at[0, 0].set(k * 0.044715).at[0, 1].set(k)


def _gelu_bf16(x, k3, k1):
    t = x * (k3 * (x * x) + k1)
    return x * (1.0 / (1.0 + jnp.exp2(t)))


def _dot(a, b):
    return jnp.dot(a, b, preferred_element_type=F32)


def _dot_nt(a, b):
    return lax.dot_general(a, b, (((1,), (1,)), ((), ())), preferred_element_type=F32)


def _dot_tn(a, b):
    return lax.dot_general(a, b, (((0,), (0,)), ((), ())), preferred_element_type=F32)


def _dot_f32(a, b):
    return jnp.dot(a, b, preferred_element_type=F32, precision=lax.Precision.HIGHEST)


def _inproj_kernel(x_ref, g_ref, w_ref, wga_ref, o_ref, ga_ref, u_scr):
    @pl.when(pl.program_id(1) == 0)
    def _():
        u_scr[...] = _rms(x_ref[...], g_ref[...]).astype(BF16)
        ga_ref[...] = _dot(u_scr[...], wga_ref[...])

    o_ref[...] = _dot(u_scr[...], w_ref[...]).astype(BF16)


def _inproj(x2, g, w_main, w_ga):
    T = x2.shape[0]
    N = w_main.shape[1]
    tm = min(1024, T)
    tn = 1024
    return pl.pallas_call(
        _inproj_kernel,
        grid=(T // tm, N // tn),
        in_specs=[
            pl.BlockSpec((tm, D_MODEL), lambda i, j: (i, 0)),
            pl.BlockSpec((1, D_MODEL), lambda i, j: (0, 0)),
            pl.BlockSpec((D_MODEL, tn), lambda i, j: (0, j)),
            pl.BlockSpec((D_MODEL, LANES), lambda i, j: (0, 0)),
        ],
        out_specs=[
            pl.BlockSpec((tm, tn), lambda i, j: (i, j)),
            pl.BlockSpec((tm, LANES), lambda i, j: (i, 0)),
        ],
        out_shape=[
            jax.ShapeDtypeStruct((T, N), BF16),
            jax.ShapeDtypeStruct((T, LANES), F32),
        ],
        scratch_shapes=[pltpu.VMEM((tm, D_MODEL), BF16)],
        compiler_params=_params("parallel", "arbitrary"),
        name="inproj",
    )(x2, g, w_main, w_ga)


def _gla_kernel(q_ref, k_ref, v_ref, r_ref, ga_ref, wup_ref, ba_ref, go_ref, o_ref, kd_scr, dec_scr, st_scr):
    S = q_ref.shape[1]
    G = GLA_GROUP
    per = G // CHUNK
    row = lax.broadcasted_iota(jnp.int32, (G, G), 0)
    col = lax.broadcasted_iota(jnp.int32, (G, G), 1)
    same = (row // CHUNK) == (col // CHUNK)
    cm = jnp.concatenate([jnp.where(same & (col <= row), 1.0, 0.0), jnp.where(same, 1.0, 0.0)], axis=0).astype(BF16)

    pre = _dot_f32(ga_ref[0], wup_ref[...]) + ba_ref[...]
    log_a = _log_sigmoid(pre) * (1.0 / GLA_TAU)
    hi = log_a.astype(BF16)
    lo = (log_a - hi.astype(F32)).astype(BF16)
    hl = jnp.concatenate([hi, lo], axis=1)
    for g in range(S // G):
        rows = slice(g * G, (g + 1) * G)
        sums = _dot(cm, hl[rows])
        sums = sums[:, :GLA_DK] + sums[:, GLA_DK:]
        logb, tot = sums[:G], sums[G:]
        kd_scr[rows, :] = (k_ref[0, rows, :].astype(F32) * jnp.exp(tot - logb)).astype(BF16)
        dec_scr[rows, :] = jnp.exp(tot)

    def walk(gi, st):
        base = pl.multiple_of(gi * G, G)
        kv = [_dot_tn(v_ref[0, pl.ds(base + c * CHUNK, CHUNK), :], kd_scr[pl.ds(base + c * CHUNK, CHUNK), :])
              for c in range(per)]
        for c in range(per):
            st = dec_scr[pl.ds(base + c * CHUNK, 1), :] * st + kv[c]
            st_scr[gi * per + c] = st.astype(BF16)
        return st

    lax.fori_loop(0, S // G, walk, jnp.zeros((GLA_DV, GLA_DK), F32))

    def emit(gi, _):
        base = pl.multiple_of(gi * G, G)
        o = jnp.concatenate([_dot_nt(q_ref[0, pl.ds(base + c * CHUNK, CHUNK), :], st_scr[gi * per + c])
                             for c in range(per)], axis=0) * (GLA_DK ** -0.5)
        o = _rms(o, go_ref[...])
        rows = pl.ds(base, G)
        r = r_ref[0, rows, :].astype(F32)
        o_ref[0, rows, :] = (o * (r * _sigmoid(r))).astype(BF16)
        return 0

    lax.fori_loop(0, S // G, emit, 0)


def _gla(proj3, ga3, wup, ba, go):
    B, S, _ = proj3.shape
    H = GLA_HEADS
    kb = (H * GLA_DK) // GLA_DK
    vb = (2 * H * GLA_DK) // GLA_DV
    rb = vb + H
    return pl.pallas_call(
        _gla_kernel,
        grid=(B, H),
        in_specs=[
            pl.BlockSpec((1, S, GLA_DK), lambda b, h: (b, 0, h)),
            pl.BlockSpec((1, S, GLA_DK), lambda b, h: (b, 0, kb + h)),
            pl.BlockSpec((1, S, GLA_DV), lambda b, h: (b, 0, vb + h)),
            pl.BlockSpec((1, S, GLA_DV), lambda b, h: (b, 0, rb + h)),
            pl.BlockSpec((1, S, LANES), lambda b, h: (b, 0, 0)),
            pl.BlockSpec((LANES, GLA_DK), lambda b, h: (0, h)),
            pl.BlockSpec((1, GLA_DK), lambda b, h: (0, h)),
            pl.BlockSpec((1, GLA_DV), lambda b, h: (0, h)),
        ],
        out_specs=pl.BlockSpec((1, S, GLA_DV), lambda b, h: (b, 0, h)),
        out_shape=jax.ShapeDtypeStruct((B, S, H * GLA_DV), BF16),
        scratch_shapes=[
            pltpu.VMEM((S, GLA_DK), BF16),
            pltpu.VMEM((S, GLA_DK), F32),
            pltpu.VMEM((S // CHUNK, GLA_DV, GLA_DK), BF16),
        ],
        compiler_params=_params("parallel", "parallel"),
        name="gla",
    )(proj3, proj3, proj3, proj3, ga3, wup, ba, go)


SB_PAIRS = 8


def _sb_kernel(q_ref, k_ref, v_ref, o_ref, acc_ref, car_ref):
    i = pl.program_id(2)
    Q = SB_BLOCK
    nch = 2 * SB_PAIRS
    row = lax.broadcasted_iota(jnp.int32, (Q, Q), 0)
    col = lax.broadcasted_iota(jnp.int32, (Q, Q), 1)
    lane = lax.broadcasted_iota(jnp.int32, (Q, LANES), 1)
    uo = jnp.concatenate([jnp.where(row > col, 1.0, 0.0), jnp.ones((Q, Q), F32)], axis=1).astype(BF16)
    uo2 = jnp.concatenate([uo, uo], axis=0)
    lower = lane < SB_DH
    srow = lax.broadcasted_iota(jnp.int32, (nch * Q, Q), 0)
    scol = lax.broadcasted_iota(jnp.int32, (nch * Q, Q), 1)

    qs = []
    for p in range(SB_PAIRS):
        q2 = q_ref[0, :, p * LANES:(p + 1) * LANES] * (SB_DH ** -0.5)
        zero = jnp.zeros_like(q2)
        qs.append(jnp.where(lower, q2, zero))
        qs.append(jnp.where(lower, zero, q2))

    def block(j, diag):
        rows = pl.ds(pl.multiple_of(j * Q, Q), Q)
        kj = k_ref[0, rows, :]
        vj = v_ref[0, rows, :]
        z = jnp.concatenate([_dot_nt(qs[c], kj[:, (c // 2) * LANES:(c // 2 + 1) * LANES])
                             for c in range(nch)], axis=0)
        ls = _log_sigmoid(z)
        stay = ls - z
        if diag:
            mask = scol < (srow & (Q - 1))
            stay = jnp.where(mask, stay, 0.0)
        hi = stay.astype(BF16)
        lo = (stay - hi.astype(F32)).astype(BF16)
        sums = _dot(jnp.concatenate([hi, lo], axis=1), uo2)
        if diag:
            w = jnp.where(mask, jnp.exp(ls + sums[:, :Q]), 0.0)
            car = sums[:, Q:]
        else:
            w = jnp.exp(ls + sums[:, :Q] + car_ref[...])
            car = car_ref[...] + sums[:, Q:]
        car_ref[...] = car
        wb = w.astype(BF16)
        pv = jnp.concatenate([_dot(wb[c * Q:(c + 1) * Q], vj[:, (c // 2) * LANES:(c // 2 + 1) * LANES])
                              for c in range(nch)], axis=0)
        if diag:
            acc_ref[...] = pv
        else:
            acc_ref[...] += pv
        return jnp.max(car)

    worst0 = block(i, True)

    def cond(s):
        j, worst = s
        return (j >= 0) & (worst > SB_EXIT)

    def body(s):
        j, _ = s
        return j - 1, block(j, False)

    lax.while_loop(cond, body, (i - 1, worst0))

    for p in range(SB_PAIRS):
        even = acc_ref[2 * p * Q:(2 * p + 1) * Q, :]
        odd = acc_ref[(2 * p + 1) * Q:(2 * p + 2) * Q, :]
        o_ref[0, :, p * LANES:(p + 1) * LANES] = jnp.where(lower, even, odd).astype(BF16)


def _sb(proj3):
    B, S, _ = proj3.shape
    W = SB_PAIRS * LANES
    base = (2 * GLA_HEADS * GLA_DK + 2 * GLA_HEADS * GLA_DV) // W
    nb = (SB_HEADS * SB_DH) // W
    return pl.pallas_call(
        _sb_kernel,
        grid=(B, nb, S // SB_BLOCK),
        in_specs=[
            pl.BlockSpec((1, SB_BLOCK, W), lambda b, p, i: (b, i, base + p)),
            pl.BlockSpec((1, S, W), lambda b, p, i: (b, 0, base + nb + p)),
            pl.BlockSpec((1, S, W), lambda b, p, i: (b, 0, base + 2 * nb + p)),
        ],
        out_specs=pl.BlockSpec((1, SB_BLOCK, W), lambda b, p, i: (b, i, p)),
        out_shape=jax.ShapeDtypeStruct((B, S, SB_HEADS * SB_DH), BF16),
        scratch_shapes=[
            pltpu.VMEM((2 * SB_PAIRS * SB_BLOCK, LANES), F32),
            pltpu.VMEM((2 * SB_PAIRS * SB_BLOCK, SB_BLOCK), F32),
        ],
        compiler_params=_params("parallel", "parallel", "arbitrary"),
        name="sb",
    )(proj3, proj3, proj3)


def _mix_kernel(gla_ref, sb_ref, ga_ref, gb_ref, x_ref, wa_ref, wb_ref, wo_ref, g_ref, h_ref, xt_ref):
    ya = _dot(gla_ref[...], wa_ref[...])
    yb = _dot(sb_ref[...], wb_ref[...])
    mixed = _sigmoid(ga_ref[...].astype(F32)) * ya + _sigmoid(gb_ref[...].astype(F32)) * yb
    h = x_ref[...] + _dot(mixed.astype(BF16), wo_ref[...])
    h_ref[...] = h
    xt_ref[...] = _rms(h, g_ref[...]).T.astype(BF16)


def _mix(gla_o, sb_o, proj, x2, wa, wb, wo, g):
    T = x2.shape[0]
    tm = min(512, T)
    D = D_MODEL
    gate_blk = (proj.shape[1] - 2 * D) // D
    full = lambda i: (0, 0)
    return pl.pallas_call(
        _mix_kernel,
        grid=(T // tm,),
        in_specs=[
            pl.BlockSpec((tm, D), lambda i: (i, 0)),
            pl.BlockSpec((tm, D), lambda i: (i, 0)),
            pl.BlockSpec((tm, D), lambda i: (i, gate_blk)),
            pl.BlockSpec((tm, D), lambda i: (i, gate_blk + 1)),
            pl.BlockSpec((tm, D), lambda i: (i, 0)),
            pl.BlockSpec((D, D), full),
            pl.BlockSpec((D, D), full),
            pl.BlockSpec((D, D), full),
            pl.BlockSpec((1, D), full),
        ],
        out_specs=[
            pl.BlockSpec((tm, D), lambda i: (i, 0)),
            pl.BlockSpec((D, tm), lambda i: (0, i)),
        ],
        out_shape=[
            jax.ShapeDtypeStruct((T, D), F32),
            jax.ShapeDtypeStruct((D, T), BF16),
        ],
        compiler_params=_params("parallel"),
        name="mix",
    )(gla_o, sb_o, proj, proj, x2, wa, wb, wo, g)


def _sort16_network():
    pairs = []

    def merge(lo, n, r):
        step = 2 * r
        if step < n:
            merge(lo, n, step)
            merge(lo + r, n, step)
            pairs.extend((i, i + r) for i in range(lo + r, lo + n - r, step))
        else:
            pairs.append((lo, lo + r))

    def sort(lo, n):
        if n > 1:
            sort(lo, n // 2)
            sort(lo + n // 2, n // 2)
            merge(lo, n, 1)

    sort(0, 16)
    return tuple(pairs)


_SORT16 = _sort16_network()
_BITONIC16 = tuple((i, i + d) for d in (8, 4, 2, 1) for i in range(16) if not i & d)


def _exchange(x, pairs):
    for i, j in pairs:
        x[i], x[j] = jnp.maximum(x[i], x[j]), jnp.minimum(x[i], x[j])


def _merge_sublanes(x):
    for shift in (4, 2, 1):
        y = [pltpu.roll(v, shift, 0) for v in x]
        x = [jnp.maximum(x[i], y[15 - i]) for i in range(16)]
        _exchange(x, _BITONIC16)
    return x


def _route_kernel(xt_ref, wq_ref, k1_ref, k2_ref, cnt_ref, e1_ref, rk_ref, e2_ref, q_scr):
    K = PEER_TOPK
    tt = xt_ref.shape[1]
    sub = lax.broadcasted_iota(jnp.int32, (8, tt), 0)
    q_scr[...] = _dot(wq_ref[...], xt_ref[...]).astype(BF16)

    def head(h, _):
        qt = q_scr[pl.ds(pl.multiple_of(h * 2 * PEER_HALF, 2 * PEER_HALF), 2 * PEER_HALF), :]
        s1 = _dot(k1_ref[h], qt[:PEER_HALF])
        s2 = _dot(k2_ref[h], qt[PEER_HALF:])
        s1 = [s1[8 * k:8 * k + 8] for k in range(16)]
        s2 = [s2[8 * k:8 * k + 8] for k in range(16)]
        v1, v2 = list(s1), list(s2)
        _exchange(v1, _SORT16)
        _exchange(v2, _SORT16)
        v1 = _merge_sublanes(v1)
        v2 = _merge_sublanes(v2)

        lo8, hi8 = v1[0], v1[8]
        for a in range(1, 8):
            lo8 = jnp.where(sub == a, v1[a], lo8)
            hi8 = jnp.where(sub == a, v1[8 + a], hi8)
        p = [lo8 + v2[0]] + [jnp.where(sub < K // (b + 1), lo8 + v2[b], NEG_INF) for b in range(1, K)]
        ph = hi8 + v2[0]
        top = list(p)
        top[K - 1] = jnp.maximum(top[K - 1], ph)
        _exchange(top, _BITONIC16)
        top = _merge_sublanes(top)
        thr = top[K - 1]
        z = jnp.ones_like(thr)
        for i in range(1, K):
            z = z + jnp.exp(top[i] - top[0])
        inv_z = 1.0 / z

        pairs_lo = jnp.where(p[0] >= thr, 1.0, 0.0)
        for b in range(1, K):
            pairs_lo = pairs_lo + jnp.where(p[b] >= thr, 1.0, 0.0)
        pairs_hi = jnp.where(ph >= thr, 1.0, 0.0)
        pairs = [jnp.broadcast_to((pairs_lo if a < 8 else pairs_hi)[a % 8:a % 8 + 1], (8, tt)) for a in range(K)]

        for k in range(0, 16, 2):
            cnt, rk = [], []
            for kk in (k, k + 1):
                c = jnp.zeros((8, tt), F32)
                r = jnp.zeros((8, tt), F32)
                for a in range(K):
                    c = jnp.where(s1[kk] == v1[a], pairs[a], c)
                    r = jnp.where(v2[a] > s2[kk], a + 1.0, r)
                cnt.append(c)
                rk.append(r)
            rows = slice(8 * k, 8 * k + 16)
            cnt_ref[h, rows, :] = jnp.concatenate(cnt, axis=0)
            rk_ref[h, rows, :] = jnp.concatenate(rk, axis=0).astype(BF16)
            e1_ref[h, rows, :] = jnp.concatenate([jnp.exp(s1[k] - v1[0]) * inv_z,
                                                  jnp.exp(s1[k + 1] - v1[0]) * inv_z], axis=0)
            e2_ref[h, rows, :] = jnp.concatenate([jnp.exp(s2[k] - v2[0]),
                                                  jnp.exp(s2[k + 1] - v2[0])], axis=0).astype(BF16)
        return 0

    lax.fori_loop(0, PEER_HEADS, head, 0)


def _route(xt, wq_t, k1, k2):
    D, T = xt.shape
    H, N = PEER_HEADS, PEER_NKEYS
    tt = min(512, T)
    return pl.pallas_call(
        _route_kernel,
        grid=(T // tt,),
        in_specs=[
            pl.BlockSpec((D, tt), lambda i: (0, i)),
            pl.BlockSpec(wq_t.shape, lambda i: (0, 0)),
            pl.BlockSpec(k1.shape, lambda i: (0, 0, 0)),
            pl.BlockSpec(k2.shape, lambda i: (0, 0, 0)),
        ],
        out_specs=[pl.BlockSpec((H, N, tt), lambda i: (0, 0, i))] * 4,
        out_shape=[
            jax.ShapeDtypeStruct((H, N, T), F32),
            jax.ShapeDtypeStruct((H, N, T), F32),
            jax.ShapeDtypeStruct((H, N, T), BF16),
            jax.ShapeDtypeStruct((H, N, T), BF16),
        ],
        scratch_shapes=[pltpu.VMEM((wq_t.shape[0], tt), BF16)],
        compiler_params=_params("parallel"),
        name="route",
    )(xt, wq_t, k1, k2)


PEER_NI = 4
PEER_TT = 1024


def _peer_kernel(xt_ref, u_ref, vt_ref, cnt_cur, e1_cur, cnt_prev, e1_prev, rk_ref, e2_ref, gk_ref, o_ref,
                 act0, act1, a0, a1):
    s = pl.program_id(1)
    N = PEER_NKEYS
    te = PEER_NI * N
    k3 = gk_ref[0:1, 0:1].astype(BF16)
    k1 = gk_ref[0:1, 1:2].astype(BF16)

    @pl.when(s == 0)
    def _():
        o_ref[...] = jnp.zeros_like(o_ref)
        act1[...] = jnp.zeros_like(act1)
        a0[...] = jnp.zeros_like(a0)

    def scores(half, part, act_ref):
        rows = slice(part * (te // 2), (part + 1) * (te // 2))
        act_ref[rows, :] = _dot(u_ref[half * te:(half + 1) * te, :][rows], xt_ref[...])

    def gates(cnt_ref, e1_ref, half, n, act_ref, a_ref):
        W = 2 * LANES
        for lg in range(xt_ref.shape[1] // W):
            ls = slice(lg * W, (lg + 1) * W)
            row = slice(half * PEER_NI + n, half * PEER_NI + n + 1)
            g = jnp.zeros((N, W), BF16)
            for h in range(PEER_HEADS):
                cb = jnp.broadcast_to(cnt_ref[h, row, ls], g.shape).astype(BF16)
                eb = jnp.broadcast_to(e1_ref[h, row, ls], g.shape).astype(BF16)
                g = g + jnp.where(rk_ref[h, :, ls] < cb, e2_ref[h, :, ls], jnp.zeros((), BF16)) * eb
            a_ref[n * N:(n + 1) * N, ls] = _gelu_bf16(act_ref[n * N:(n + 1) * N, ls].astype(BF16), k3, k1) * g

    def mix(half, a_ref):
        o_ref[...] += _dot(vt_ref[:, half * te:(half + 1) * te], a_ref[...])

    scores(0, 0, act0)
    gates(cnt_prev, e1_prev, 1, 0, act1, a1)
    gates(cnt_prev, e1_prev, 1, 1, act1, a1)
    mix(0, a0)
    gates(cnt_prev, e1_prev, 1, 2, act1, a1)
    scores(0, 1, act0)
    gates(cnt_prev, e1_prev, 1, 3, act1, a1)

    scores(1, 0, act1)
    gates(cnt_cur, e1_cur, 0, 0, act0, a0)
    gates(cnt_cur, e1_cur, 0, 1, act0, a0)
    mix(1, a1)
    gates(cnt_cur, e1_cur, 0, 2, act0, a0)
    scores(1, 1, act1)
    gates(cnt_cur, e1_cur, 0, 3, act0, a0)


def _peer(xt, u_bf, vt_bf, cnt, e1, rk, e2):
    D, T = xt.shape
    H, N = PEER_HEADS, PEER_NKEYS
    tt = min(PEER_TT, T)
    te = PEER_NI * N
    nblk = (N * N) // (2 * te)
    cur = lambda i, s: (0, jnp.minimum(s, nblk - 1), i)
    prev = lambda i, s: (0, jnp.maximum(s - 1, 0), i)
    return pl.pallas_call(
        _peer_kernel,
        grid=(T // tt, nblk + 1),
        in_specs=[
            pl.BlockSpec((D, tt), lambda i, s: (0, i)),
            pl.BlockSpec((2 * te, D), lambda i, s: (jnp.minimum(s, nblk - 1), 0)),
            pl.BlockSpec((D, 2 * te), lambda i, s: (0, jnp.maximum(s - 1, 0))),
            pl.BlockSpec((H, 2 * PEER_NI, tt), cur),
            pl.BlockSpec((H, 2 * PEER_NI, tt), cur),
            pl.BlockSpec((H, 2 * PEER_NI, tt), prev),
            pl.BlockSpec((H, 2 * PEER_NI, tt), prev),
            pl.BlockSpec((H, N, tt), lambda i, s: (0, 0, i)),
            pl.BlockSpec((H, N, tt), lambda i, s: (0, 0, i)),
            pl.BlockSpec((8, LANES), lambda i, s: (0, 0)),
        ],
        out_specs=pl.BlockSpec((D, tt), lambda i, s: (0, i)),
        out_shape=jax.ShapeDtypeStruct((D, T), F32),
        scratch_shapes=[pltpu.VMEM((te, tt), F32)] * 2 + [pltpu.VMEM((te, tt), BF16)] * 2,
        compiler_params=_params("parallel", "arbitrary"),
        name="peer",
    )(xt, u_bf, vt_bf, cnt, e1, cnt, e1, rk, e2, _gelu_consts())


def _final_kernel(h_ref, pt_ref, p_ref, wp_ref, wg_ref, gp_ref, gf_ref, o_ref):
    h2 = h_ref[...] + pt_ref[...].T
    ple = _dot(p_ref[...].astype(BF16), wp_ref[...])
    gate = _sigmoid(_dot(_rms(h2, gp_ref[...]).astype(BF16), wg_ref[...]))
    o_ref[...] = _rms(h2 + ple * gate, gf_ref[...])


def _final(h1, peer_t, p2, wp, wg, gp, gf):
    T, D = h1.shape
    tm = min(512, T)
    full = lambda i: (0, 0)
    return pl.pallas_call(
        _final_kernel,
        grid=(T // tm,),
        in_specs=[
            pl.BlockSpec((tm, D), lambda i: (i, 0)),
            pl.BlockSpec((D, tm), lambda i: (0, i)),
            pl.BlockSpec((tm, PLE_DIM), lambda i: (i, 0)),
            pl.BlockSpec((PLE_DIM, D), full),
            pl.BlockSpec((D, D), full),
            pl.BlockSpec((1, D), full),
            pl.BlockSpec((1, D), full),
        ],
        out_specs=pl.BlockSpec((tm, D), lambda i: (i, 0)),
        out_shape=jax.ShapeDtypeStruct((T, D), F32),
        compiler_params=_params("parallel"),
        name="final",
    )(h1, peer_t, p2, wp, wg, gp, gf)


def kernel(x, p, g_mix, w_in, w_gla_a_up, b_gla_a, g_gla_o, w_gla_out, w_sb_out, w_o, g_ffn, w_peer_q, peer_k1, peer_k2, peer_u, peer_v, g_ple, w_ple_gate, w_ple, g_final):
    B, S, D = x.shape
    T = B * S
    depth = w_in.shape[0]
    assert depth == 1, "the final RMSNorm is fused into the last stage of a single layer"
    qk = GLA_HEADS * GLA_DK
    gv = GLA_HEADS * GLA_DV
    lo = 2 * qk + 2 * gv
    h = x.reshape(T, D)
    for i in range(depth):
        w_main = jnp.concatenate([w_in[i][:, :lo], w_in[i][:, lo + GLA_RANK:]], axis=1).astype(BF16)
        w_ga = jnp.pad(w_in[i][:, lo:lo + GLA_RANK], ((0, 0), (0, LANES - GLA_RANK))).astype(BF16)
        wup = jnp.pad(w_gla_a_up[i], ((0, LANES - GLA_RANK), (0, 0)))

        proj, a_lo = _inproj(h, g_mix[i][None], w_main, w_ga)
        proj3 = proj.reshape(B, S, -1)
        gla_o = _gla(proj3, a_lo.reshape(B, S, LANES), wup, b_gla_a[i][None], g_gla_o[i][None])
        sb_o = _sb(proj3)
        h1, xt = _mix(gla_o.reshape(T, -1), sb_o.reshape(T, -1), proj, h,
                      w_gla_out[i].astype(BF16), w_sb_out[i].astype(BF16), w_o[i].astype(BF16), g_ffn[i][None])
        cnt, e1, rk, e2 = _route(xt, w_peer_q[i].T.astype(BF16), peer_k1[i].astype(BF16), peer_k2[i].astype(BF16))
        peer_t = _peer(xt, peer_u[i].astype(BF16), peer_v[i].T.astype(BF16), cnt, e1, rk, e2)
        h = _final(h1, peer_t, p[i].reshape(T, -1), w_ple[i].astype(BF16), w_ple_gate[i].astype(BF16),
                   g_ple[i][None], g_final[None])
    return h.reshape(B, S, D)
```

```python
import functools
import math

import jax
import jax.numpy as jnp
from jax import lax
from jax.experimental import pallas as pl
from jax.experimental.pallas import tpu as pltpu

F32 = jnp.float32
BF16 = jnp.bfloat16

D_MODEL = 1024
EPS = 1e-6
CHUNK = 64
GLA_HEADS, GLA_DK, GLA_DV, GLA_RANK, GLA_TAU = 4, 128, 256, 16, 16.0
SB_HEADS, SB_DH, SB_BLOCK = 16, 64, 128
PEER_HEADS, PEER_NKEYS, PEER_TOPK = 8, 128, 16
PEER_HALF = 128
PLE_DIM = 256

LANES = 128
VMEM_LIMIT = 56 * 1024 * 1024

SB_EXIT = -64.0

GLA_GROUP = 4 * CHUNK

NEG_INF = float("-inf")


def _params(*sem):
    return pltpu.CompilerParams(dimension_semantics=sem, vmem_limit_bytes=VMEM_LIMIT)


def _rms(x, g):
    return x * lax.rsqrt(jnp.mean(x * x, axis=-1, keepdims=True) + EPS) * g


def _sigmoid(x):
    return 1.0 / (1.0 + jnp.exp(-x))


def _log_sigmoid(x):
    return jnp.minimum(x, 0.0) - jnp.log(1.0 + jnp.exp(-jnp.abs(x)))


def _gelu_tanh(x):
    c = math.sqrt(2.0 / math.pi)
    return 0.5 * x * (1.0 + jnp.tanh(c * (x + 0.044715 * (x * x * x))))


def _gelu_consts():
    k = -2.0 * math.sqrt(2.0 / math.pi) * math.log2(math.e)
    return jnp.zeros((8, LANES), F32).at[0, 0].set(k * 0.044715).at[0, 1].set(k)


def _gelu_bf16(x, k3, k1):
    t = x * (k3 * (x * x) + k1)
    return x * (1.0 / (1.0 + jnp.exp2(t)))


def _dot(a, b):
    return jnp.dot(a, b, preferred_element_type=F32)


def _dot_nt(a, b):
    return lax.dot_general(a, b, (((1,), (1,)), ((), ())), preferred_element_type=F32)


def _dot_tn(a, b):
    return lax.dot_general(a, b, (((0,), (0,)), ((), ())), preferred_element_type=F32)


def _dot_f32(a, b):
    return jnp.dot(a, b, preferred_element_type=F32, precision=lax.Precision.HIGHEST)


def _inproj_kernel(x_ref, g_ref, w_ref, wga_ref, o_ref, ga_ref, u_scr):
    @pl.when(pl.program_id(1) == 0)
    def _():
        u_scr[...] = _rms(x_ref[...], g_ref[...]).astype(BF16)
        ga_ref[...] = _dot(u_scr[...], wga_ref[...])

    o_ref[...] = _dot(u_scr[...], w_ref[...]).astype(BF16)


def _inproj(x2, g, w_main, w_ga):
    T = x2.shape[0]
    N = w_main.shape[1]
    tm = min(1024, T)
    tn = 1024
    return pl.pallas_call(
        _inproj_kernel,
        grid=(T // tm, N // tn),
        in_specs=[
            pl.BlockSpec((tm, D_MODEL), lambda i, j: (i, 0)),
            pl.BlockSpec((1, D_MODEL), lambda i, j: (0, 0)),
            pl.BlockSpec((D_MODEL, tn), lambda i, j: (0, j)),
            pl.BlockSpec((D_MODEL, LANES), lambda i, j: (0, 0)),
        ],
        out_specs=[
            pl.BlockSpec((tm, tn), lambda i, j: (i, j)),
            pl.BlockSpec((tm, LANES), lambda i, j: (i, 0)),
        ],
        out_shape=[
            jax.ShapeDtypeStruct((T, N), BF16),
            jax.ShapeDtypeStruct((T, LANES), F32),
        ],
        scratch_shapes=[pltpu.VMEM((tm, D_MODEL), BF16)],
        compiler_params=_params("parallel", "arbitrary"),
        name="inproj",
    )(x2, g, w_main, w_ga)


def _gla_kernel(q_ref, k_ref, v_ref, r_ref, ga_ref, wup_ref, ba_ref, go_ref, o_ref, kd_scr, dec_scr, st_scr):
    S = q_ref.shape[1]
    G = GLA_GROUP
    per = G // CHUNK
    row = lax.broadcasted_iota(jnp.int32, (G, G), 0)
    col = lax.broadcasted_iota(jnp.int32, (G, G), 1)
    same = (row // CHUNK) == (col // CHUNK)
    cm = jnp.concatenate([jnp.where(same & (col <= row), 1.0, 0.0), jnp.where(same, 1.0, 0.0)], axis=0).astype(BF16)

    pre = _dot_f32(ga_ref[0], wup_ref[...]) + ba_ref[...]
    log_a = _log_sigmoid(pre) * (1.0 / GLA_TAU)
    hi = log_a.astype(BF16)
    lo = (log_a - hi.astype(F32)).astype(BF16)
    hl = jnp.concatenate([hi, lo], axis=1)
    for g in range(S // G):
        rows = slice(g * G, (g + 1) * G)
        sums = _dot(cm, hl[rows])
        sums = sums[:, :GLA_DK] + sums[:, GLA_DK:]
        logb, tot = sums[:G], sums[G:]
        kd_scr[rows, :] = (k_ref[0, rows, :].astype(F32) * jnp.exp(tot - logb)).astype(BF16)
        dec_scr[rows, :] = jnp.exp(tot)

    def walk(gi, st):
        base = pl.multiple_of(gi * G, G)
        kv = [_dot_tn(v_ref[0, pl.ds(base + c * CHUNK, CHUNK), :], kd_scr[pl.ds(base + c * CHUNK, CHUNK), :])
              for c in range(per)]
        for c in range(per):
            st = dec_scr[pl.ds(base + c * CHUNK, 1), :] * st + kv[c]
            st_scr[gi * per + c] = st.astype(BF16)
        return st

    lax.fori_loop(0, S // G, walk, jnp.zeros((GLA_DV, GLA_DK), F32))

    def emit(gi, _):
        base = pl.multiple_of(gi * G, G)
        o = jnp.concatenate([_dot_nt(q_ref[0, pl.ds(base + c * CHUNK, CHUNK), :], st_scr[gi * per + c])
                             for c in range(per)], axis=0) * (GLA_DK ** -0.5)
        o = _rms(o, go_ref[...])
        rows = pl.ds(base, G)
        r = r_ref[0, rows, :].astype(F32)
        o_ref[0, rows, :] = (o * (r * _sigmoid(r))).astype(BF16)
        return 0

    lax.fori_loop(0, S // G, emit, 0)


def _gla(proj3, ga3, wup, ba, go):
    B, S, _ = proj3.shape
    H = GLA_HEADS
    kb = (H * GLA_DK) // GLA_DK
    vb = (2 * H * GLA_DK) // GLA_DV
    rb = vb + H
    return pl.pallas_call(
        _gla_kernel,
        grid=(B, H),
        in_specs=[
            pl.BlockSpec((1, S, GLA_DK), lambda b, h: (b, 0, h)),
            pl.BlockSpec((1, S, GLA_DK), lambda b, h: (b, 0, kb + h)),
            pl.BlockSpec((1, S, GLA_DV), lambda b, h: (b, 0, vb + h)),
            pl.BlockSpec((1, S, GLA_DV), lambda b, h: (b, 0, rb + h)),
            pl.BlockSpec((1, S, LANES), lambda b, h: (b, 0, 0)),
            pl.BlockSpec((LANES, GLA_DK), lambda b, h: (0, h)),
            pl.BlockSpec((1, GLA_DK), lambda b, h: (0, h)),
            pl.BlockSpec((1, GLA_DV), lambda b, h: (0, h)),
        ],
        out_specs=pl.BlockSpec((1, S, GLA_DV), lambda b, h: (b, 0, h)),
        out_shape=jax.ShapeDtypeStruct((B, S, H * GLA_DV), BF16),
        scratch_shapes=[
            pltpu.VMEM((S, GLA_DK), BF16),
            pltpu.VMEM((S, GLA_DK), F32),
            pltpu.VMEM((S // CHUNK, GLA_DV, GLA_DK), BF16),
        ],
        compiler_params=_params("parallel", "parallel"),
        name="gla",
    )(proj3, proj3, proj3, proj3, ga3, wup, ba, go)


SB_PAIRS = 8


def _sb_kernel(q_ref, k_ref, v_ref, o_ref, acc_ref, car_ref):
    i = pl.program_id(2)
    Q = SB_BLOCK
    nch = 2 * SB_PAIRS
    row = lax.broadcasted_iota(jnp.int32, (Q, Q), 0)
    col = lax.broadcasted_iota(jnp.int32, (Q, Q), 1)
    lane = lax.broadcasted_iota(jnp.int32, (Q, LANES), 1)
    uo = jnp.concatenate([jnp.where(row > col, 1.0, 0.0), jnp.ones((Q, Q), F32)], axis=1).astype(BF16)
    uo2 = jnp.concatenate([uo, uo], axis=0)
    lower = lane < SB_DH
    srow = lax.broadcasted_iota(jnp.int32, (nch * Q, Q), 0)
    scol = lax.broadcasted_iota(jnp.int32, (nch * Q, Q), 1)

    qs = []
    for p in range(SB_PAIRS):
        q2 = q_ref[0, :, p * LANES:(p + 1) * LANES] * (SB_DH ** -0.5)
        zero = jnp.zeros_like(q2)
        qs.append(jnp.where(lower, q2, zero))
        qs.append(jnp.where(lower, zero, q2))

    def block(j, diag):
        rows = pl.ds(pl.multiple_of(j * Q, Q), Q)
        kj = k_ref[0, rows, :]
        vj = v_ref[0, rows, :]
        z = jnp.concatenate([_dot_nt(qs[c], kj[:, (c // 2) * LANES:(c // 2 + 1) * LANES])
                             for c in range(nch)], axis=0)
        ls = _log_sigmoid(z)
        stay = ls - z
        if diag:
            mask = scol < (srow & (Q - 1))
            stay = jnp.where(mask, stay, 0.0)
        hi = stay.astype(BF16)
        lo = (stay - hi.astype(F32)).astype(BF16)
        sums = _dot(jnp.concatenate([hi, lo], axis=1), uo2)
        if diag:
            w = jnp.where(mask, jnp.exp(ls + sums[:, :Q]), 0.0)
            car = sums[:, Q:]
        else:
            w = jnp.exp(ls + sums[:, :Q] + car_ref[...])
            car = car_ref[...] + sums[:, Q:]
        car_ref[...] = car
        wb = w.astype(BF16)
        pv = jnp.concatenate([_dot(wb[c * Q:(c + 1) * Q], vj[:, (c // 2) * LANES:(c // 2 + 1) * LANES])
                              for c in range(nch)], axis=0)
        if diag:
            acc_ref[...] = pv
        else:
            acc_ref[...] += pv
        return jnp.max(car)

    worst0 = block(i, True)

    def cond(s):
        j, worst = s
        return (j >= 0) & (worst > SB_EXIT)

    def body(s):
        j, _ = s
        return j - 1, block(j, False)

    lax.while_loop(cond, body, (i - 1, worst0))

    for p in range(SB_PAIRS):
        even = acc_ref[2 * p * Q:(2 * p + 1) * Q, :]
        odd = acc_ref[(2 * p + 1) * Q:(2 * p + 2) * Q, :]
        o_ref[0, :, p * LANES:(p + 1) * LANES] = jnp.where(lower, even, odd).astype(BF16)


def _sb(proj3):
    B, S, _ = proj3.shape
    W = SB_PAIRS * LANES
    base = (2 * GLA_HEADS * GLA_DK + 2 * GLA_HEADS * GLA_DV) // W
    nb = (SB_HEADS * SB_DH) // W
    return pl.pallas_call(
        _sb_kernel,
        grid=(B, nb, S // SB_BLOCK),
        in_specs=[
            pl.BlockSpec((1, SB_BLOCK, W), lambda b, p, i: (b, i, base + p)),
            pl.BlockSpec((1, S, W), lambda b, p, i: (b, 0, base + nb + p)),
            pl.BlockSpec((1, S, W), lambda b, p, i: (b, 0, base + 2 * nb + p)),
        ],
        out_specs=pl.BlockSpec((1, SB_BLOCK, W), lambda b, p, i: (b, i, p)),
        out_shape=jax.ShapeDtypeStruct((B, S, SB_HEADS * SB_DH), BF16),
        scratch_shapes=[
            pltpu.VMEM((2 * SB_PAIRS * SB_BLOCK, LANES), F32),
            pltpu.VMEM((2 * SB_PAIRS * SB_BLOCK, SB_BLOCK), F32),
        ],
        compiler_params=_params("parallel", "parallel", "arbitrary"),
        name="sb",
    )(proj3, proj3, proj3)


def _mix_kernel(gla_ref, sb_ref, ga_ref, gb_ref, x_ref, wa_ref, wb_ref, wo_ref, g_ref, h_ref, xt_ref):
    ya = _dot(gla_ref[...], wa_ref[...])
    yb = _dot(sb_ref[...], wb_ref[...])
    mixed = _sigmoid(ga_ref[...].astype(F32)) * ya + _sigmoid(gb_ref[...].astype(F32)) * yb
    h = x_ref[...] + _dot(mixed.astype(BF16), wo_ref[...])
    h_ref[...] = h
    xt_ref[...] = _rms(h, g_ref[...]).T.astype(BF16)


def _mix(gla_o, sb_o, proj, x2, wa, wb, wo, g):
    T = x2.shape[0]
    tm = min(512, T)
    D = D_MODEL
    gate_blk = (proj.shape[1] - 2 * D) // D
    full = lambda i: (0, 0)
    return pl.pallas_call(
        _mix_kernel,
        grid=(T // tm,),
        in_specs=[
            pl.BlockSpec((tm, D), lambda i: (i, 0)),
            pl.BlockSpec((tm, D), lambda i: (i, 0)),
            pl.BlockSpec((tm, D), lambda i: (i, gate_blk)),
            pl.BlockSpec((tm, D), lambda i: (i, gate_blk + 1)),
            pl.BlockSpec((tm, D), lambda i: (i, 0)),
            pl.BlockSpec((D, D), full),
            pl.BlockSpec((D, D), full),
            pl.BlockSpec((D, D), full),
            pl.BlockSpec((1, D), full),
        ],
        out_specs=[
            pl.BlockSpec((tm, D), lambda i: (i, 0)),
            pl.BlockSpec((D, tm), lambda i: (0, i)),
        ],
        out_shape=[
            jax.ShapeDtypeStruct((T, D), F32),
            jax.ShapeDtypeStruct((D, T), BF16),
        ],
        compiler_params=_params("parallel"),
        name="mix",
    )(gla_o, sb_o, proj, proj, x2, wa, wb, wo, g)


def _sort16_network():
    pairs = []

    def merge(lo, n, r):
        step = 2 * r
        if step < n:
            merge(lo, n, step)
            merge(lo + r, n, step)
            pairs.extend((i, i + r) for i in range(lo + r, lo + n - r, step))
        else:
            pairs.append((lo, lo + r))

    def sort(lo, n):
        if n > 1:
            sort(lo, n // 2)
            sort(lo + n // 2, n // 2)
            merge(lo, n, 1)

    sort(0, 16)
    return tuple(pairs)


_SORT16 = _sort16_network()
_BITONIC16 = tuple((i, i + d) for d in (8, 4, 2, 1) for i in range(16) if not i & d)


def _exchange(x, pairs):
    for i, j in pairs:
        x[i], x[j] = jnp.maximum(x[i], x[j]), jnp.minimum(x[i], x[j])


def _merge_sublanes(x):
    for shift in (4, 2, 1):
        y = [pltpu.roll(v, shift, 0) for v in x]
        x = [jnp.maximum(x[i], y[15 - i]) for i in range(16)]
        _exchange(x, _BITONIC16)
    return x


def _route_kernel(xt_ref, wq_ref, k1_ref, k2_ref, cnt_ref, e1_ref, rk_ref, e2_ref, q_scr):
    K = PEER_TOPK
    tt = xt_ref.shape[1]
    sub = lax.broadcasted_iota(jnp.int32, (8, tt), 0)
    q_scr[...] = _dot(wq_ref[...], xt_ref[...]).astype(BF16)

    def head(h, _):
        qt = q_scr[pl.ds(pl.multiple_of(h * 2 * PEER_HALF, 2 * PEER_HALF), 2 * PEER_HALF), :]
        s1 = _dot(k1_ref[h], qt[:PEER_HALF])
        s2 = _dot(k2_ref[h], qt[PEER_HALF:])
        s1 = [s1[8 * k:8 * k + 8] for k in range(16)]
        s2 = [s2[8 * k:8 * k + 8] for k in range(16)]
        v1, v2 = list(s1), list(s2)
        _exchange(v1, _SORT16)
        _exchange(v2, _SORT16)
        v1 = _merge_sublanes(v1)
        v2 = _merge_sublanes(v2)

        lo8, hi8 = v1[0], v1[8]
        for a in range(1, 8):
            lo8 = jnp.where(sub == a, v1[a], lo8)
            hi8 = jnp.where(sub == a, v1[8 + a], hi8)
        p = [lo8 + v2[0]] + [jnp.where(sub < K // (b + 1), lo8 + v2[b], NEG_INF) for b in range(1, K)]
        ph = hi8 + v2[0]
        top = list(p)
        top[K - 1] = jnp.maximum(top[K - 1], ph)
        _exchange(top, _BITONIC16)
        top = _merge_sublanes(top)
        thr = top[K - 1]
        z = jnp.ones_like(thr)
        for i in range(1, K):
            z = z + jnp.exp(top[i] - top[0])
        inv_z = 1.0 / z

        pairs_lo = jnp.where(p[0] >= thr, 1.0, 0.0)
        for b in range(1, K):
            pairs_lo = pairs_lo + jnp.where(p[b] >= thr, 1.0, 0.0)
        pairs_hi = jnp.where(ph >= thr, 1.0, 0.0)
        pairs = [jnp.broadcast_to((pairs_lo if a < 8 else pairs_hi)[a % 8:a % 8 + 1], (8, tt)) for a in range(K)]

        for k in range(0, 16, 2):
            cnt, rk = [], []
            for kk in (k, k + 1):
                c = jnp.zeros((8, tt), F32)
                r = jnp.zeros((8, tt), F32)
                for a in range(K):
                    c = jnp.where(s1[kk] == v1[a], pairs[a], c)
                    r = jnp.where(v2[a] > s2[kk], a + 1.0, r)
                cnt.append(c)
                rk.append(r)
            rows = slice(8 * k, 8 * k + 16)
            cnt_ref[h, rows, :] = jnp.concatenate(cnt, axis=0)
            rk_ref[h, rows, :] = jnp.concatenate(rk, axis=0).astype(BF16)
            e1_ref[h, rows, :] = jnp.concatenate([jnp.exp(s1[k] - v1[0]) * inv_z,
                                                  jnp.exp(s1[k + 1] - v1[0]) * inv_z], axis=0)
            e2_ref[h, rows, :] = jnp.concatenate([jnp.exp(s2[k] - v2[0]),
                                                  jnp.exp(s2[k + 1] - v2[0])], axis=0).astype(BF16)
        return 0

    lax.fori_loop(0, PEER_HEADS, head, 0)


def _route(xt, wq_t, k1, k2):
    D, T = xt.shape
    H, N = PEER_HEADS, PEER_NKEYS
    tt = min(512, T)
    return pl.pallas_call(
        _route_kernel,
        grid=(T // tt,),
        in_specs=[
            pl.BlockSpec((D, tt), lambda i: (0, i)),
            pl.BlockSpec(wq_t.shape, lambda i: (0, 0)),
            pl.BlockSpec(k1.shape, lambda i: (0, 0, 0)),
            pl.BlockSpec(k2.shape, lambda i: (0, 0, 0)),
        ],
        out_specs=[pl.BlockSpec((H, N, tt), lambda i: (0, 0, i))] * 4,
        out_shape=[
            jax.ShapeDtypeStruct((H, N, T), F32),
            jax.ShapeDtypeStruct((H, N, T), F32),
            jax.ShapeDtypeStruct((H, N, T), BF16),
            jax.ShapeDtypeStruct((H, N, T), BF16),
        ],
        scratch_shapes=[pltpu.VMEM((wq_t.shape[0], tt), BF16)],
        compiler_params=_params("parallel"),
        name="route",
    )(xt, wq_t, k1, k2)


PEER_NI = 4
PEER_TT = 1024


def _peer_kernel(xt_ref, u_ref, vt_ref, cnt_cur, e1_cur, cnt_prev, e1_prev, rk_ref, e2_ref, gk_ref, o_ref,
                 act0, act1, a0, a1):
    s = pl.program_id(1)
    N = PEER_NKEYS
    te = PEER_NI * N
    k3 = gk_ref[0:1, 0:1].astype(BF16)
    k1 = gk_ref[0:1, 1:2].astype(BF16)

    def scores(half, part, act_ref):
        rows = slice(part * (te // 2), (part + 1) * (te // 2))
        act_ref[rows, :] = _dot(u_ref[half * te:(half + 1) * te, :][rows], xt_ref[...])

    def gates(cnt_ref, e1_ref, half, n, act_ref, a_ref):
        W = 2 * LANES
        for lg in range(xt_ref.shape[1] // W):
            ls = slice(lg * W, (lg + 1) * W)
            row = slice(half * PEER_NI + n, half * PEER_NI + n + 1)
            g = jnp.zeros((N, W), BF16)
            for h in range(PEER_HEADS):
                cb = jnp.broadcast_to(cnt_ref[h, row, ls], g.shape).astype(BF16)
                eb = jnp.broadcast_to(e1_ref[h, row, ls], g.shape).astype(BF16)
                g = g + jnp.where(rk_ref[h, :, ls] < cb, e2_ref[h, :, ls], jnp.zeros((), BF16)) * eb
            a_ref[n * N:(n + 1) * N, ls] = _gelu_bf16(act_ref[n * N:(n + 1) * N, ls].astype(BF16), k3, k1) * g

    def mix(half, a_ref):
        o_ref[...] += _dot(vt_ref[:, half * te:(half + 1) * te], a_ref[...])

    last = pl.num_programs(1) - 1

    @pl.when(s == 0)
    def _():
        o_ref[...] = jnp.zeros_like(o_ref)
        scores(0, 0, act0)
        scores(0, 1, act0)
        scores(1, 0, act1)
        gates(cnt_cur, e1_cur, 0, 0, act0, a0)
        gates(cnt_cur, e1_cur, 0, 1, act0, a0)
        scores(1, 1, act1)
        gates(cnt_cur, e1_cur, 0, 2, act0, a0)
        gates(cnt_cur, e1_cur, 0, 3, act0, a0)

    @pl.when((s > 0) & (s < last))
    def _():
        scores(0, 0, act0)
        gates(cnt_prev, e1_prev, 1, 0, act1, a1)
        gates(cnt_prev, e1_prev, 1, 1, act1, a1)
        mix(0, a0)
        gates(cnt_prev, e1_prev, 1, 2, act1, a1)
        scores(0, 1, act0)
        gates(cnt_prev, e1_prev, 1, 3, act1, a1)

        scores(1, 0, act1)
        gates(cnt_cur, e1_cur, 0, 0, act0, a0)
        gates(cnt_cur, e1_cur, 0, 1, act0, a0)
        mix(1, a1)
        gates(cnt_cur, e1_cur, 0, 2, act0, a0)
        scores(1, 1, act1)
        gates(cnt_cur, e1_cur, 0, 3, act0, a0)

    @pl.when(s == last)
    def _():
        gates(cnt_prev, e1_prev, 1, 0, act1, a1)
        gates(cnt_prev, e1_prev, 1, 1, act1, a1)
        mix(0, a0)
        gates(cnt_prev, e1_prev, 1, 2, act1, a1)
        gates(cnt_prev, e1_prev, 1, 3, act1, a1)
        mix(1, a1)


def _peer(xt, u_bf, vt_bf, cnt, e1, rk, e2):
    D, T = xt.shape
    H, N = PEER_HEADS, PEER_NKEYS
    tt = min(PEER_TT, T)
    te = PEER_NI * N
    nblk = (N * N) // (2 * te)
    cur = lambda i, s: (0, jnp.minimum(s, nblk - 1), i)
    prev = lambda i, s: (0, jnp.maximum(s - 1, 0), i)
    return pl.pallas_call(
        _peer_kernel,
        grid=(T // tt, nblk + 1),
        in_specs=[
            pl.BlockSpec((D, tt), lambda i, s: (0, i)),
            pl.BlockSpec((2 * te, D), lambda i, s: (jnp.minimum(s, nblk - 1), 0)),
            pl.BlockSpec((D, 2 * te), lambda i, s: (0, jnp.maximum(s - 1, 0))),
            pl.BlockSpec((H, 2 * PEER_NI, tt), cur),
            pl.BlockSpec((H, 2 * PEER_NI, tt), cur),
            pl.BlockSpec((H, 2 * PEER_NI, tt), prev),
            pl.BlockSpec((H, 2 * PEER_NI, tt), prev),
            pl.BlockSpec((H, N, tt), lambda i, s: (0, 0, i)),
            pl.BlockSpec((H, N, tt), lambda i, s: (0, 0, i)),
            pl.BlockSpec((8, LANES), lambda i, s: (0, 0)),
        ],
        out_specs=pl.BlockSpec((D, tt), lambda i, s: (0, i)),
        out_shape=jax.ShapeDtypeStruct((D, T), F32),
        scratch_shapes=[pltpu.VMEM((te, tt), F32)] * 2 + [pltpu.VMEM((te, tt), BF16)] * 2,
        compiler_params=_params("parallel", "arbitrary"),
        name="peer",
    )(xt, u_bf, vt_bf, cnt, e1, cnt, e1, rk, e2, _gelu_consts())


def _final_kernel(h_ref, pt_ref, p_ref, wp_ref, wg_ref, gp_ref, gf_ref, o_ref):
    h2 = h_ref[...] + pt_ref[...].T
    ple = _dot(p_ref[...].astype(BF16), wp_ref[...])
    gate = _sigmoid(_dot(_rms(h2, gp_ref[...]).astype(BF16), wg_ref[...]))
    o_ref[...] = _rms(h2 + ple * gate, gf_ref[...])


def _final(h1, peer_t, p2, wp, wg, gp, gf):
    T, D = h1.shape
    tm = min(512, T)
    full = lambda i: (0, 0)
    return pl.pallas_call(
        _final_kernel,
        grid=(T // tm,),
        in_specs=[
            pl.BlockSpec((tm, D), lambda i: (i, 0)),
            pl.BlockSpec((D, tm), lambda i: (0, i)),
            pl.BlockSpec((tm, PLE_DIM), lambda i: (i, 0)),
            pl.BlockSpec((PLE_DIM, D), full),
            pl.BlockSpec((D, D), full),
            pl.BlockSpec((1, D), full),
            pl.BlockSpec((1, D), full),
        ],
        out_specs=pl.BlockSpec((tm, D), lambda i: (i, 0)),
        out_shape=jax.ShapeDtypeStruct((T, D), F32),
        compiler_params=_params("parallel"),
        name="final",
    )(h1, peer_t, p2, wp, wg, gp, gf)


def kernel(x, p, g_mix, w_in, w_gla_a_up, b_gla_a, g_gla_o, w_gla_out, w_sb_out, w_o, g_ffn, w_peer_q, peer_k1, peer_k2, peer_u, peer_v, g_ple, w_ple_gate, w_ple, g_final):
    B, S, D = x.shape
    T = B * S
    depth = w_in.shape[0]
    assert depth == 1, "the final RMSNorm is fused into the last stage of a single layer"
    qk = GLA_HEADS * GLA_DK
    gv = GLA_HEADS * GLA_DV
    lo = 2 * qk + 2 * gv
    h = x.reshape(T, D)
    for i in range(depth):
        w_main = jnp.concatenate([w_in[i][:, :lo], w_in[i][:, lo + GLA_RANK:]], axis=1).astype(BF16)
        w_ga = jnp.pad(w_in[i][:, lo:lo + GLA_RANK], ((0, 0), (0, LANES - GLA_RANK))).astype(BF16)
        wup = jnp.pad(w_gla_a_up[i], ((0, LANES - GLA_RANK), (0, 0)))

        proj, a_lo = _inproj(h, g_mix[i][None], w_main, w_ga)
        proj3 = proj.reshape(B, S, -1)
        gla_o = _gla(proj3, a_lo.reshape(B, S, LANES), wup, b_gla_a[i][None], g_gla_o[i][None])
        sb_o = _sb(proj3)
        h1, xt = _mix(gla_o.reshape(T, -1), sb_o.reshape(T, -1), proj, h,
                      w_gla_out[i].astype(BF16), w_sb_out[i].astype(BF16), w_o[i].astype(BF16), g_ffn[i][None])
        cnt, e1, rk, e2 = _route(xt, w_peer_q[i].T.astype(BF16), peer_k1[i].astype(BF16), peer_k2[i].astype(BF16))
        peer_t = _peer(xt, peer_u[i].astype(BF16), peer_v[i].T.astype(BF16), cnt, e1, rk, e2)
        h = _final(h1, peer_t, p[i].reshape(T, -1), w_ple[i].astype(BF16), w_ple_gate[i].astype(BF16),
                   g_ple[i][None], g_final[None])
    return h.reshape(B, S, D)
```

```python
import functools
import math

import jax
import jax.numpy as jnp
from jax import lax
from jax.experimental import pallas as pl
from jax.experimental.pallas import tpu as pltpu

F32 = jnp.float32
BF16 = jnp.bfloat16

D_MODEL = 1024
EPS = 1e-6
CHUNK = 64
GLA_HEADS, GLA_DK, GLA_DV, GLA_RANK, GLA_TAU = 4, 128, 256, 16, 16.0
SB_HEADS, SB_DH, SB_BLOCK = 16, 64, 128
PEER_HEADS, PEER_NKEYS, PEER_TOPK = 8, 128, 16
PEER_HALF = 128
PLE_DIM = 256

LANES = 128
VMEM_LIMIT = 56 * 1024 * 1024

SB_EXIT = -64.0

GLA_GROUP = 4 * CHUNK

NEG_INF = float("-inf")


def _params(*sem):
    return pltpu.CompilerParams(dimension_semantics=sem, vmem_limit_bytes=VMEM_LIMIT)


def _rms(x, g):
    return x * lax.rsqrt(jnp.mean(x * x, axis=-1, keepdims=True) + EPS) * g


def _sigmoid(x):
    return 1.0 / (1.0 + jnp.exp(-x))


def _log_sigmoid(x):
    return jnp.minimum(x, 0.0) - jnp.log(1.0 + jnp.exp(-jnp.abs(x)))


def _gelu_tanh(x):
    c = math.sqrt(2.0 / math.pi)
    return 0.5 * x * (1.0 + jnp.tanh(c * (x + 0.044715 * (x * x * x))))


def _gelu_consts():
    k = -2.0 * math.sqrt(2.0 / math.pi) * math.log2(math.e)
    return jnp.zeros((8, LANES), F32).at[0, 0].set(k * 0.044715).at[0, 1].set(k)


def _gelu_bf16(x, k3, k1):
    t = x * (k3 * (x * x) + k1)
    return x * (1.0 / (1.0 + jnp.exp2(t)))


def _dot(a, b):
    return jnp.dot(a, b, preferred_element_type=F32)


def _dot_nt(a, b):
    return lax.dot_general(a, b, (((1,), (1,)), ((), ())), preferred_element_type=F32)


def _dot_tn(a, b):
    return lax.dot_general(a, b, (((0,), (0,)), ((), ())), preferred_element_type=F32)


def _inproj_kernel(x_ref, g_ref, w_ref, wga_ref, o_ref, ga_ref, u_scr):
    @pl.when(pl.program_id(1) == 0)
    def _():
        u_scr[...] = _rms(x_ref[...], g_ref[...]).astype(BF16)
        ga_ref[...] = _dot(u_scr[...], wga_ref[...])

    o_ref[...] = _dot(u_scr[...], w_ref[...]).astype(BF16)


def _inproj(x2, g, w_main, w_ga):
    T = x2.shape[0]
    N = w_main.shape[1]
    tm = min(1024, T)
    tn = 1024
    return pl.pallas_call(
        _inproj_kernel,
        grid=(T // tm, N // tn),
        in_specs=[
            pl.BlockSpec((tm, D_MODEL), lambda i, j: (i, 0)),
            pl.BlockSpec((1, D_MODEL), lambda i, j: (0, 0)),
            pl.BlockSpec((D_MODEL, tn), lambda i, j: (0, j)),
            pl.BlockSpec((D_MODEL, LANES), lambda i, j: (0, 0)),
        ],
        out_specs=[
            pl.BlockSpec((tm, tn), lambda i, j: (i, j)),
            pl.BlockSpec((tm, LANES), lambda i, j: (i, 0)),
        ],
        out_shape=[
            jax.ShapeDtypeStruct((T, N), BF16),
            jax.ShapeDtypeStruct((T, LANES), F32),
        ],
        scratch_shapes=[pltpu.VMEM((tm, D_MODEL), BF16)],
        compiler_params=_params("parallel", "arbitrary"),
        name="inproj",
    )(x2, g, w_main, w_ga)


def _gla_kernel(q_ref, k_ref, v_ref, r_ref, ga_ref, wup_ref, ba_ref, go_ref, o_ref, kd_scr, dec_scr, st_scr):
    S = q_ref.shape[1]
    G = GLA_GROUP
    row = lax.broadcasted_iota(jnp.int32, (G, G), 0)
    col = lax.broadcasted_iota(jnp.int32, (G, G), 1)
    same = (row // CHUNK) == (col // CHUNK)
    cm = jnp.concatenate([jnp.where(same & (col <= row), 1.0, 0.0), jnp.where(same, 1.0, 0.0)], axis=0).astype(BF16)

    ga = ga_ref[0]
    ga_hi = ga.astype(BF16)
    ga_lo = (ga - ga_hi.astype(F32)).astype(BF16)
    wu = wup_ref[...]
    wu_hi = wu.astype(BF16)
    wu_lo = (wu - wu_hi.astype(F32)).astype(BF16)
    pre = (_dot(jnp.concatenate([ga_hi, ga_lo], axis=1), jnp.concatenate([wu_hi, wu_hi], axis=0))
           + _dot(ga_hi, wu_lo) + ba_ref[...])
    log_a = _log_sigmoid(pre) * (1.0 / GLA_TAU)
    hi = log_a.astype(BF16)
    lo = (log_a - hi.astype(F32)).astype(BF16)
    hl = jnp.concatenate([hi, lo], axis=1)
    for g in range(S // G):
        rows = slice(g * G, (g + 1) * G)
        sums = _dot(cm, hl[rows])
        sums = sums[:, :GLA_DK] + sums[:, GLA_DK:]
        logb, tot = sums[:G], sums[G:]
        kd_scr[rows, :] = (k_ref[0, rows, :].astype(F32) * jnp.exp(tot - logb)).astype(BF16)
        dec_scr[rows, :] = jnp.exp(tot)

    WG = 2 * G
    per = WG // CHUNK

    def walk(gi, st):
        base = pl.multiple_of(gi * WG, WG)
        kv = [_dot_tn(v_ref[0, pl.ds(base + c * CHUNK, CHUNK), :], kd_scr[pl.ds(base + c * CHUNK, CHUNK), :])
              for c in range(per)]
        for c in range(per):
            st = dec_scr[pl.ds(base + c * CHUNK, 1), :] * st + kv[c]
            st_scr[gi * per + c] = st.astype(BF16)
        return st

    lax.fori_loop(0, S // WG, walk, jnp.zeros((GLA_DV, GLA_DK), F32))

    def emit(gi, _):
        base = pl.multiple_of(gi * WG, WG)
        o = jnp.concatenate([_dot_nt(q_ref[0, pl.ds(base + c * CHUNK, CHUNK), :], st_scr[gi * per + c])
                             for c in range(per)], axis=0) * (GLA_DK ** -0.5)
        o = _rms(o, go_ref[...])
        rows = pl.ds(base, WG)
        r = r_ref[0, rows, :].astype(F32)
        o_ref[0, rows, :] = (o * (r * _sigmoid(r))).astype(BF16)
        return 0

    lax.fori_loop(0, S // WG, emit, 0)


def _gla(proj3, ga3, wup, ba, go):
    B, S, _ = proj3.shape
    H = GLA_HEADS
    kb = (H * GLA_DK) // GLA_DK
    vb = (2 * H * GLA_DK) // GLA_DV
    rb = vb + H
    return pl.pallas_call(
        _gla_kernel,
        grid=(B, H),
        in_specs=[
            pl.BlockSpec((1, S, GLA_DK), lambda b, h: (b, 0, h)),
            pl.BlockSpec((1, S, GLA_DK), lambda b, h: (b, 0, kb + h)),
            pl.BlockSpec((1, S, GLA_DV), lambda b, h: (b, 0, vb + h)),
            pl.BlockSpec((1, S, GLA_DV), lambda b, h: (b, 0, rb + h)),
            pl.BlockSpec((1, S, LANES), lambda b, h: (b, 0, 0)),
            pl.BlockSpec((LANES, GLA_DK), lambda b, h: (0, h)),
            pl.BlockSpec((1, GLA_DK), lambda b, h: (0, h)),
            pl.BlockSpec((1, GLA_DV), lambda b, h: (0, h)),
        ],
        out_specs=pl.BlockSpec((1, S, GLA_DV), lambda b, h: (b, 0, h)),
        out_shape=jax.ShapeDtypeStruct((B, S, H * GLA_DV), BF16),
        scratch_shapes=[
            pltpu.VMEM((S, GLA_DK), BF16),
            pltpu.VMEM((S, GLA_DK), F32),
            pltpu.VMEM((S // CHUNK, GLA_DV, GLA_DK), BF16),
        ],
        compiler_params=_params("parallel", "parallel"),
        name="gla",
    )(proj3, proj3, proj3, proj3, ga3, wup, ba, go)


SB_PAIRS = 8


def _sb_kernel(q_ref, k_ref, v_ref, o_ref, acc_ref, car_ref):
    i = pl.program_id(2)
    Q = SB_BLOCK
    nch = 2 * SB_PAIRS
    row = lax.broadcasted_iota(jnp.int32, (Q, Q), 0)
    col = lax.broadcasted_iota(jnp.int32, (Q, Q), 1)
    lane = lax.broadcasted_iota(jnp.int32, (Q, LANES), 1)
    uo = jnp.concatenate([jnp.where(row > col, 1.0, 0.0), jnp.ones((Q, Q), F32)], axis=1).astype(BF16)
    uo2 = jnp.concatenate([uo, uo], axis=0)
    lower = lane < SB_DH
    srow = lax.broadcasted_iota(jnp.int32, (nch * Q, Q), 0)
    scol = lax.broadcasted_iota(jnp.int32, (nch * Q, Q), 1)

    qs = []
    for p in range(SB_PAIRS):
        q2 = q_ref[0, :, p * LANES:(p + 1) * LANES] * (SB_DH ** -0.5)
        zero = jnp.zeros_like(q2)
        qs.append(jnp.where(lower, q2, zero))
        qs.append(jnp.where(lower, zero, q2))

    def block(j, diag):
        rows = pl.ds(pl.multiple_of(j * Q, Q), Q)
        kj = k_ref[0, rows, :]
        vj = v_ref[0, rows, :]
        z = jnp.concatenate([_dot_nt(qs[c], kj[:, (c // 2) * LANES:(c // 2 + 1) * LANES])
                             for c in range(nch)], axis=0)
        ls = _log_sigmoid(z)
        stay = ls - z
        if diag:
            mask = scol < (srow & (Q - 1))
            stay = jnp.where(mask, stay, 0.0)
        hi = stay.astype(BF16)
        lo = (stay - hi.astype(F32)).astype(BF16)
        sums = _dot(jnp.concatenate([hi, lo], axis=1), uo2)
        if diag:
            w = jnp.where(mask, jnp.exp(ls + sums[:, :Q]), 0.0)
            car = sums[:, Q:]
        else:
            w = jnp.exp(ls + sums[:, :Q] + car_ref[...])
            car = car_ref[...] + sums[:, Q:]
        car_ref[...] = car
        wb = w.astype(BF16)
        pv = jnp.concatenate([_dot(wb[c * Q:(c + 1) * Q], vj[:, (c // 2) * LANES:(c // 2 + 1) * LANES])
                              for c in range(nch)], axis=0)
        if diag:
            acc_ref[...] = pv
        else:
            acc_ref[...] += pv
        return jnp.max(car)

    worst0 = block(i, True)

    def cond(s):
        j, worst = s
        return (j >= 0) & (worst > SB_EXIT)

    def body(s):
        j, _ = s
        return j - 1, block(j, False)

    lax.while_loop(cond, body, (i - 1, worst0))

    for p in range(SB_PAIRS):
        even = acc_ref[2 * p * Q:(2 * p + 1) * Q, :]
        odd = acc_ref[(2 * p + 1) * Q:(2 * p + 2) * Q, :]
        o_ref[0, :, p * LANES:(p + 1) * LANES] = jnp.where(lower, even, odd).astype(BF16)


def _sb(proj3):
    B, S, _ = proj3.shape
    W = SB_PAIRS * LANES
    base = (2 * GLA_HEADS * GLA_DK + 2 * GLA_HEADS * GLA_DV) // W
    nb = (SB_HEADS * SB_DH) // W
    return pl.pallas_call(
        _sb_kernel,
        grid=(B, nb, S // SB_BLOCK),
        in_specs=[
            pl.BlockSpec((1, SB_BLOCK, W), lambda b, p, i: (b, i, base + p)),
            pl.BlockSpec((1, S, W), lambda b, p, i: (b, 0, base + nb + p)),
            pl.BlockSpec((1, S, W), lambda b, p, i: (b, 0, base + 2 * nb + p)),
        ],
        out_specs=pl.BlockSpec((1, SB_BLOCK, W), lambda b, p, i: (b, i, p)),
        out_shape=jax.ShapeDtypeStruct((B, S, SB_HEADS * SB_DH), BF16),
        scratch_shapes=[
            pltpu.VMEM((2 * SB_PAIRS * SB_BLOCK, LANES), F32),
            pltpu.VMEM((2 * SB_PAIRS * SB_BLOCK, SB_BLOCK), F32),
        ],
        compiler_params=_params("parallel", "parallel", "arbitrary"),
        name="sb",
    )(proj3, proj3, proj3)


def _mix_kernel(gla_ref, sb_ref, ga_ref, gb_ref, x_ref, wa_ref, wb_ref, wo_ref, g_ref, h_ref, xt_ref):
    ya = _dot(gla_ref[...], wa_ref[...])
    yb = _dot(sb_ref[...], wb_ref[...])
    mixed = _sigmoid(ga_ref[...].astype(F32)) * ya + _sigmoid(gb_ref[...].astype(F32)) * yb
    h = x_ref[...] + _dot(mixed.astype(BF16), wo_ref[...])
    h_ref[...] = h
    xt_ref[...] = _rms(h, g_ref[...]).T.astype(BF16)


def _mix(gla_o, sb_o, proj, x2, wa, wb, wo, g):
    T = x2.shape[0]
    tm = min(512, T)
    D = D_MODEL
    gate_blk = (proj.shape[1] - 2 * D) // D
    full = lambda i: (0, 0)
    return pl.pallas_call(
        _mix_kernel,
        grid=(T // tm,),
        in_specs=[
            pl.BlockSpec((tm, D), lambda i: (i, 0)),
            pl.BlockSpec((tm, D), lambda i: (i, 0)),
            pl.BlockSpec((tm, D), lambda i: (i, gate_blk)),
            pl.BlockSpec((tm, D), lambda i: (i, gate_blk + 1)),
            pl.BlockSpec((tm, D), lambda i: (i, 0)),
            pl.BlockSpec((D, D), full),
            pl.BlockSpec((D, D), full),
            pl.BlockSpec((D, D), full),
            pl.BlockSpec((1, D), full),
        ],
        out_specs=[
            pl.BlockSpec((tm, D), lambda i: (i, 0)),
            pl.BlockSpec((D, tm), lambda i: (0, i)),
        ],
        out_shape=[
            jax.ShapeDtypeStruct((T, D), F32),
            jax.ShapeDtypeStruct((D, T), BF16),
        ],
        compiler_params=_params("parallel"),
        name="mix",
    )(gla_o, sb_o, proj, proj, x2, wa, wb, wo, g)


def _sort16_network():
    pairs = []

    def merge(lo, n, r):
        step = 2 * r
        if step < n:
            merge(lo, n, step)
            merge(lo + r, n, step)
            pairs.extend((i, i + r) for i in range(lo + r, lo + n - r, step))
        else:
            pairs.append((lo, lo + r))

    def sort(lo, n):
        if n > 1:
            sort(lo, n // 2)
            sort(lo + n // 2, n // 2)
            merge(lo, n, 1)

    sort(0, 16)
    return tuple(pairs)


_SORT16 = _sort16_network()
_BITONIC16 = tuple((i, i + d) for d in (8, 4, 2, 1) for i in range(16) if not i & d)


def _exchange(x, pairs):
    for i, j in pairs:
        x[i], x[j] = jnp.maximum(x[i], x[j]), jnp.minimum(x[i], x[j])


def _merge_sublanes(x):
    for shift in (4, 2, 1):
        y = [pltpu.roll(v, shift, 0) for v in x]
        x = [jnp.maximum(x[i], y[15 - i]) for i in range(16)]
        _exchange(x, _BITONIC16)
    return x


def _route_kernel(xt_ref, wq_ref, k1_ref, k2_ref, cnt_ref, e1_ref, rk_ref, e2_ref, q_scr):
    K = PEER_TOPK
    tt = xt_ref.shape[1]
    sub = lax.broadcasted_iota(jnp.int32, (8, tt), 0)
    q_scr[...] = _dot(wq_ref[...], xt_ref[...]).astype(BF16)

    def head(h, _):
        qt = q_scr[pl.ds(pl.multiple_of(h * 2 * PEER_HALF, 2 * PEER_HALF), 2 * PEER_HALF), :]
        s1 = _dot(k1_ref[h], qt[:PEER_HALF])
        s2 = _dot(k2_ref[h], qt[PEER_HALF:])
        s1 = [s1[8 * k:8 * k + 8] for k in range(16)]
        s2 = [s2[8 * k:8 * k + 8] for k in range(16)]
        v1, v2 = list(s1), list(s2)
        _exchange(v1, _SORT16)
        _exchange(v2, _SORT16)
        v1 = _merge_sublanes(v1)
        v2 = _merge_sublanes(v2)

        lo8, hi8 = v1[0], v1[8]
        for a in range(1, 8):
            lo8 = jnp.where(sub == a, v1[a], lo8)
            hi8 = jnp.where(sub == a, v1[8 + a], hi8)
        p = [lo8 + v2[0]] + [jnp.where(sub < K // (b + 1), lo8 + v2[b], NEG_INF) for b in range(1, K)]
        ph = hi8 + v2[0]
        top = list(p)
        top[K - 1] = jnp.maximum(top[K - 1], ph)
        _exchange(top, _BITONIC16)
        top = _merge_sublanes(top)
        thr = top[K - 1]
        z = jnp.ones_like(thr)
        for i in range(1, K):
            z = z + jnp.exp(top[i] - top[0])
        inv_z = 1.0 / z

        pairs_lo = jnp.where(p[0] >= thr, 1.0, 0.0)
        for b in range(1, K):
            pairs_lo = pairs_lo + jnp.where(p[b] >= thr, 1.0, 0.0)
        pairs_hi = jnp.where(ph >= thr, 1.0, 0.0)
        pairs = [jnp.broadcast_to((pairs_lo if a < 8 else pairs_hi)[a % 8:a % 8 + 1], (8, tt)) for a in range(K)]

        for k in range(0, 16, 2):
            cnt, rk = [], []
            for kk in (k, k + 1):
                c = jnp.zeros((8, tt), F32)
                r = jnp.zeros((8, tt), F32)
                for a in range(K):
                    c = jnp.where(s1[kk] == v1[a], pairs[a], c)
                    r = jnp.where(v2[a] > s2[kk], a + 1.0, r)
                cnt.append(c)
                rk.append(r)
            rows = slice(8 * k, 8 * k + 16)
            cnt_ref[h, rows, :] = jnp.concatenate(cnt, axis=0)
            rk_ref[h, rows, :] = jnp.concatenate(rk, axis=0).astype(BF16)
            e1_ref[h, rows, :] = jnp.concatenate([jnp.exp(s1[k] - v1[0]) * inv_z,
                                                  jnp.exp(s1[k + 1] - v1[0]) * inv_z], axis=0)
            e2_ref[h, rows, :] = jnp.concatenate([jnp.exp(s2[k] - v2[0]),
                                                  jnp.exp(s2[k + 1] - v2[0])], axis=0).astype(BF16)
        return 0

    lax.fori_loop(0, PEER_HEADS, head, 0)


def _route(xt, wq_t, k1, k2):
    D, T = xt.shape
    H, N = PEER_HEADS, PEER_NKEYS
    tt = min(512, T)
    return pl.pallas_call(
        _route_kernel,
        grid=(T // tt,),
        in_specs=[
            pl.BlockSpec((D, tt), lambda i: (0, i)),
            pl.BlockSpec(wq_t.shape, lambda i: (0, 0)),
            pl.BlockSpec(k1.shape, lambda i: (0, 0, 0)),
            pl.BlockSpec(k2.shape, lambda i: (0, 0, 0)),
        ],
        out_specs=[pl.BlockSpec((H, N, tt), lambda i: (0, 0, i))] * 4,
        out_shape=[
            jax.ShapeDtypeStruct((H, N, T), F32),
            jax.ShapeDtypeStruct((H, N, T), F32),
            jax.ShapeDtypeStruct((H, N, T), BF16),
            jax.ShapeDtypeStruct((H, N, T), BF16),
        ],
        scratch_shapes=[pltpu.VMEM((wq_t.shape[0], tt), BF16)],
        compiler_params=_params("parallel"),
        name="route",
    )(xt, wq_t, k1, k2)


PEER_NI = 4
PEER_TT = 1024


def _peer_kernel(xt_ref, u_ref, vt_ref, cnt_cur, e1_cur, cnt_prev, e1_prev, rk_ref, e2_ref, gk_ref, o_ref,
                 act0, act1, a0, a1):
    s = pl.program_id(1)
    N = PEER_NKEYS
    te = PEER_NI * N
    k3 = gk_ref[0:1, 0:1].astype(BF16)
    k1 = gk_ref[0:1, 1:2].astype(BF16)

    def scores(half, act_ref):
        act_ref[...] = _dot(u_ref[half * te:(half + 1) * te, :], xt_ref[...])

    def gates(cnt_ref, e1_ref, half, n, act_ref, a_ref):
        W = 2 * LANES
        for lg in range(xt_ref.shape[1] // W):
            ls = slice(lg * W, (lg + 1) * W)
            row = slice(half * PEER_NI + n, half * PEER_NI + n + 1)
            g = jnp.zeros((N, W), BF16)
            for h in range(PEER_HEADS):
                cb = jnp.broadcast_to(cnt_ref[h, row, ls], g.shape).astype(BF16)
                eb = jnp.broadcast_to(e1_ref[h, row, ls], g.shape).astype(BF16)
                g = g + jnp.where(rk_ref[h, :, ls] < cb, e2_ref[h, :, ls], jnp.zeros((), BF16)) * eb
            a_ref[n * N:(n + 1) * N, ls] = _gelu_bf16(act_ref[n * N:(n + 1) * N, ls].astype(BF16), k3, k1) * g

    def mix(half, a_ref):
        o_ref[...] += _dot(vt_ref[:, half * te:(half + 1) * te], a_ref[...])

    last = pl.num_programs(1) - 1

    @pl.when(s == 0)
    def _():
        o_ref[...] = jnp.zeros_like(o_ref)
        scores(0, act0)
        scores(1, act1)
        for n in range(PEER_NI):
            gates(cnt_cur, e1_cur, 0, n, act0, a0)

    @pl.when((s > 0) & (s < last))
    def _():
        scores(0, act0)
        gates(cnt_prev, e1_prev, 1, 0, act1, a1)
        gates(cnt_prev, e1_prev, 1, 1, act1, a1)
        mix(0, a0)
        gates(cnt_prev, e1_prev, 1, 2, act1, a1)
        gates(cnt_prev, e1_prev, 1, 3, act1, a1)

        scores(1, act1)
        gates(cnt_cur, e1_cur, 0, 0, act0, a0)
        gates(cnt_cur, e1_cur, 0, 1, act0, a0)
        mix(1, a1)
        gates(cnt_cur, e1_cur, 0, 2, act0, a0)
        gates(cnt_cur, e1_cur, 0, 3, act0, a0)

    @pl.when(s == last)
    def _():
        gates(cnt_prev, e1_prev, 1, 0, act1, a1)
        gates(cnt_prev, e1_prev, 1, 1, act1, a1)
        mix(0, a0)
        gates(cnt_prev, e1_prev, 1, 2, act1, a1)
        gates(cnt_prev, e1_prev, 1, 3, act1, a1)
        mix(1, a1)


def _peer(xt, u_bf, vt_bf, cnt, e1, rk, e2):
    D, T = xt.shape
    H, N = PEER_HEADS, PEER_NKEYS
    tt = min(PEER_TT, T)
    te = PEER_NI * N
    nblk = (N * N) // (2 * te)
    cur = lambda i, s: (0, jnp.minimum(s, nblk - 1), i)
    prev = lambda i, s: (0, jnp.maximum(s - 1, 0), i)
    return pl.pallas_call(
        _peer_kernel,
        grid=(T // tt, nblk + 1),
        in_specs=[
            pl.BlockSpec((D, tt), lambda i, s: (0, i)),
            pl.BlockSpec((2 * te, D), lambda i, s: (jnp.minimum(s, nblk - 1), 0)),
            pl.BlockSpec((D, 2 * te), lambda i, s: (0, jnp.maximum(s - 1, 0))),
            pl.BlockSpec((H, 2 * PEER_NI, tt), cur),
            pl.BlockSpec((H, 2 * PEER_NI, tt), cur),
            pl.BlockSpec((H, 2 * PEER_NI, tt), prev),
            pl.BlockSpec((H, 2 * PEER_NI, tt), prev),
            pl.BlockSpec((H, N, tt), lambda i, s: (0, 0, i)),
            pl.BlockSpec((H, N, tt), lambda i, s: (0, 0, i)),
            pl.BlockSpec((8, LANES), lambda i, s: (0, 0)),
        ],
        out_specs=pl.BlockSpec((D, tt), lambda i, s: (0, i)),
        out_shape=jax.ShapeDtypeStruct((D, T), F32),
        scratch_shapes=[pltpu.VMEM((te, tt), F32)] * 2 + [pltpu.VMEM((te, tt), BF16)] * 2,
        compiler_params=_params("parallel", "arbitrary"),
        name="peer",
    )(xt, u_bf, vt_bf, cnt, e1, cnt, e1, rk, e2, _gelu_consts())


def _final_kernel(h_ref, pt_ref, p_ref, wp_ref, wg_ref, gp_ref, gf_ref, o_ref):
    h2 = h_ref[...] + pt_ref[...].T
    ple = _dot(p_ref[...].astype(BF16), wp_ref[...])
    gate = _sigmoid(_dot(_rms(h2, gp_ref[...]).astype(BF16), wg_ref[...]))
    o_ref[...] = _rms(h2 + ple * gate, gf_ref[...])


def _final(h1, peer_t, p2, wp, wg, gp, gf):
    T, D = h1.shape
    tm = min(512, T)
    full = lambda i: (0, 0)
    return pl.pallas_call(
        _final_kernel,
        grid=(T // tm,),
        in_specs=[
            pl.BlockSpec((tm, D), lambda i: (i, 0)),
            pl.BlockSpec((D, tm), lambda i: (0, i)),
            pl.BlockSpec((tm, PLE_DIM), lambda i: (i, 0)),
            pl.BlockSpec((PLE_DIM, D), full),
            pl.BlockSpec((D, D), full),
            pl.BlockSpec((1, D), full),
            pl.BlockSpec((1, D), full),
        ],
        out_specs=pl.BlockSpec((tm, D), lambda i: (i, 0)),
        out_shape=jax.ShapeDtypeStruct((T, D), F32),
        compiler_params=_params("parallel"),
        name="final",
    )(h1, peer_t, p2, wp, wg, gp, gf)


def kernel(x, p, g_mix, w_in, w_gla_a_up, b_gla_a, g_gla_o, w_gla_out, w_sb_out, w_o, g_ffn, w_peer_q, peer_k1, peer_k2, peer_u, peer_v, g_ple, w_ple_gate, w_ple, g_final):
    B, S, D = x.shape
    T = B * S
    depth = w_in.shape[0]
    assert depth == 1, "the final RMSNorm is fused into the last stage of a single layer"
    qk = GLA_HEADS * GLA_DK
    gv = GLA_HEADS * GLA_DV
    lo = 2 * qk + 2 * gv
    h = x.reshape(T, D)
    for i in range(depth):
        w_main = jnp.concatenate([w_in[i][:, :lo], w_in[i][:, lo + GLA_RANK:]], axis=1).astype(BF16)
        w_ga = jnp.pad(w_in[i][:, lo:lo + GLA_RANK], ((0, 0), (0, LANES - GLA_RANK))).astype(BF16)
        wup = jnp.pad(w_gla_a_up[i], ((0, LANES - GLA_RANK), (0, 0)))

        proj, a_lo = _inproj(h, g_mix[i][None], w_main, w_ga)
        proj3 = proj.reshape(B, S, -1)
        gla_o = _gla(proj3, a_lo.reshape(B, S, LANES), wup, b_gla_a[i][None], g_gla_o[i][None])
        sb_o = _sb(proj3)
        h1, xt = _mix(gla_o.reshape(T, -1), sb_o.reshape(T, -1), proj, h,
                      w_gla_out[i].astype(BF16), w_sb_out[i].astype(BF16), w_o[i].astype(BF16), g_ffn[i][None])
        cnt, e1, rk, e2 = _route(xt, w_peer_q[i].astype(BF16).T, peer_k1[i].astype(BF16), peer_k2[i].astype(BF16))
        peer_t = _peer(xt, peer_u[i].astype(BF16), peer_v[i].astype(BF16).T, cnt, e1, rk, e2)
        h = _final(h1, peer_t, p[i].reshape(T, -1), w_ple[i].astype(BF16), w_ple_gate[i].astype(BF16),
                   g_ple[i][None], g_final[None])
    return h.reshape(B, S, D)
```

```python
import functools
import math

import jax
import jax.numpy as jnp
from jax import lax
from jax.experimental import pallas as pl
from jax.experimental.pallas import tpu as pltpu

F32 = jnp.float32
BF16 = jnp.bfloat16

D_MODEL = 1024
EPS = 1e-6
CHUNK = 64
GLA_HEADS, GLA_DK, GLA_DV, GLA_RANK, GLA_TAU = 4, 128, 256, 16, 16.0
SB_HEADS, SB_DH, SB_BLOCK = 16, 64, 128
PEER_HEADS, PEER_NKEYS, PEER_TOPK = 8, 128, 16
PEER_HALF = 128
PLE_DIM = 256

LANES = 128
VMEM_LIMIT = 56 * 1024 * 1024

SB_EXIT = -64.0

GLA_GROUP = 4 * CHUNK

NEG_INF = float("-inf")


def _params(*sem):
    return pltpu.CompilerParams(dimension_semantics=sem, vmem_limit_bytes=VMEM_LIMIT)


def _rms(x, g):
    return x * lax.rsqrt(jnp.mean(x * x, axis=-1, keepdims=True) + EPS) * g


def _sigmoid(x):
    return 1.0 / (1.0 + jnp.exp(-x))


def _log_sigmoid(x):
    return jnp.minimum(x, 0.0) - jnp.log(1.0 + jnp.exp(-jnp.abs(x)))


def _gelu_tanh(x):
    c = math.sqrt(2.0 / math.pi)
    return 0.5 * x * (1.0 + jnp.tanh(c * (x + 0.044715 * (x * x * x))))


def _gelu_consts():
    k = -2.0 * math.sqrt(2.0 / math.pi) * math.log2(math.e)
    return jnp.zeros((8, LANES), F32).at[0, 0].set(k * 0.044715).at[0, 1].set(k)


def _gelu_bf16(x, k3, k1):
    t = x * (k3 * (x * x) + k1)
    return x * (1.0 / (1.0 + jnp.exp2(t)))


def _dot(a, b):
    return jnp.dot(a, b, preferred_element_type=F32)


def _dot_nt(a, b):
    return lax.dot_general(a, b, (((1,), (1,)), ((), ())), preferred_element_type=F32)


def _dot_tn(a, b):
    return lax.dot_general(a, b, (((0,), (0,)), ((), ())), preferred_element_type=F32)


def _inproj_kernel(x_ref, g_ref, w_ref, wga_ref, o_ref, ga_ref, u_scr):
    @pl.when(pl.program_id(1) == 0)
    def _():
        u_scr[...] = _rms(x_ref[...], g_ref[...]).astype(BF16)
        ga_ref[...] = _dot(u_scr[...], wga_ref[...])

    o_ref[...] = _dot(u_scr[...], w_ref[...]).astype(BF16)


def _inproj(x2, g, w_main, w_ga):
    T = x2.shape[0]
    N = w_main.shape[1]
    tm = min(1024, T)
    tn = 2048
    return pl.pallas_call(
        _inproj_kernel,
        grid=(T // tm, N // tn),
        in_specs=[
            pl.BlockSpec((tm, D_MODEL), lambda i, j: (i, 0)),
            pl.BlockSpec((1, D_MODEL), lambda i, j: (0, 0)),
            pl.BlockSpec((D_MODEL, tn), lambda i, j: (0, j)),
            pl.BlockSpec((D_MODEL, LANES), lambda i, j: (0, 0)),
        ],
        out_specs=[
            pl.BlockSpec((tm, tn), lambda i, j: (i, j)),
            pl.BlockSpec((tm, LANES), lambda i, j: (i, 0)),
        ],
        out_shape=[
            jax.ShapeDtypeStruct((T, N), BF16),
            jax.ShapeDtypeStruct((T, LANES), F32),
        ],
        scratch_shapes=[pltpu.VMEM((tm, D_MODEL), BF16)],
        compiler_params=_params("parallel", "arbitrary"),
        name="inproj",
    )(x2, g, w_main, w_ga)


def _gla_kernel(q_ref, k_ref, v_ref, r_ref, ga_ref, wup_ref, ba_ref, go_ref, o_ref, kd_scr, dec_scr, st_scr):
    S = q_ref.shape[1]
    G = GLA_GROUP
    row = lax.broadcasted_iota(jnp.int32, (G, G), 0)
    col = lax.broadcasted_iota(jnp.int32, (G, G), 1)
    same = (row // CHUNK) == (col // CHUNK)
    cm = jnp.concatenate([jnp.where(same & (col <= row), 1.0, 0.0), jnp.where(same, 1.0, 0.0)], axis=0).astype(BF16)

    ga = ga_ref[0]
    ga_hi = ga.astype(BF16)
    ga_lo = (ga - ga_hi.astype(F32)).astype(BF16)
    wu = wup_ref[...]
    wu_hi = wu.astype(BF16)
    wu_lo = (wu - wu_hi.astype(F32)).astype(BF16)
    pre = (_dot(jnp.concatenate([ga_hi, ga_lo], axis=1), jnp.concatenate([wu_hi, wu_hi], axis=0))
           + _dot(ga_hi, wu_lo) + ba_ref[...])
    log_a = _log_sigmoid(pre) * (1.0 / GLA_TAU)
    hi = log_a.astype(BF16)
    lo = (log_a - hi.astype(F32)).astype(BF16)
    hl = jnp.concatenate([hi, lo], axis=1)
    for g in range(S // G):
        rows = slice(g * G, (g + 1) * G)
        sums = _dot(cm, hl[rows])
        sums = sums[:, :GLA_DK] + sums[:, GLA_DK:]
        logb, tot = sums[:G], sums[G:]
        kd_scr[rows, :] = (k_ref[0, rows, :].astype(F32) * jnp.exp(tot - logb)).astype(BF16)
        dec_scr[rows, :] = jnp.exp(tot)

    WG = 2 * G
    per = WG // CHUNK

    def walk(gi, st):
        base = pl.multiple_of(gi * WG, WG)
        kv = [_dot_tn(v_ref[0, pl.ds(base + c * CHUNK, CHUNK), :], kd_scr[pl.ds(base + c * CHUNK, CHUNK), :])
              for c in range(per)]
        for c in range(per):
            st = dec_scr[pl.ds(base + c * CHUNK, 1), :] * st + kv[c]
            st_scr[gi * per + c] = st.astype(BF16)
        return st

    lax.fori_loop(0, S // WG, walk, jnp.zeros((GLA_DV, GLA_DK), F32))

    def emit(gi, _):
        base = pl.multiple_of(gi * WG, WG)
        o = jnp.concatenate([_dot_nt(q_ref[0, pl.ds(base + c * CHUNK, CHUNK), :], st_scr[gi * per + c])
                             for c in range(per)], axis=0) * (GLA_DK ** -0.5)
        o = _rms(o, go_ref[...])
        rows = pl.ds(base, WG)
        r = r_ref[0, rows, :].astype(F32)
        o_ref[0, rows, :] = (o * (r * _sigmoid(r))).astype(BF16)
        return 0

    lax.fori_loop(0, S // WG, emit, 0)


def _gla(proj3, ga3, wup, ba, go):
    B, S, _ = proj3.shape
    H = GLA_HEADS
    kb = (H * GLA_DK) // GLA_DK
    vb = (2 * H * GLA_DK) // GLA_DV
    rb = vb + H
    return pl.pallas_call(
        _gla_kernel,
        grid=(B, H),
        in_specs=[
            pl.BlockSpec((1, S, GLA_DK), lambda b, h: (b, 0, h)),
            pl.BlockSpec((1, S, GLA_DK), lambda b, h: (b, 0, kb + h)),
            pl.BlockSpec((1, S, GLA_DV), lambda b, h: (b, 0, vb + h)),
            pl.BlockSpec((1, S, GLA_DV), lambda b, h: (b, 0, rb + h)),
            pl.BlockSpec((1, S, LANES), lambda b, h: (b, 0, 0)),
            pl.BlockSpec((LANES, GLA_DK), lambda b, h: (0, h)),
            pl.BlockSpec((1, GLA_DK), lambda b, h: (0, h)),
            pl.BlockSpec((1, GLA_DV), lambda b, h: (0, h)),
        ],
        out_specs=pl.BlockSpec((1, S, GLA_DV), lambda b, h: (b, 0, h)),
        out_shape=jax.ShapeDtypeStruct((B, S, H * GLA_DV), BF16),
        scratch_shapes=[
            pltpu.VMEM((S, GLA_DK), BF16),
            pltpu.VMEM((S, GLA_DK), F32),
            pltpu.VMEM((S // CHUNK, GLA_DV, GLA_DK), BF16),
        ],
        compiler_params=_params("parallel", "parallel"),
        name="gla",
    )(proj3, proj3, proj3, proj3, ga3, wup, ba, go)


SB_PAIRS = 8


def _sb_kernel(q_ref, k_ref, v_ref, o_ref, acc_ref, car_ref):
    i = pl.program_id(2)
    Q = SB_BLOCK
    nch = 2 * SB_PAIRS
    row = lax.broadcasted_iota(jnp.int32, (Q, Q), 0)
    col = lax.broadcasted_iota(jnp.int32, (Q, Q), 1)
    lane = lax.broadcasted_iota(jnp.int32, (Q, LANES), 1)
    uo = jnp.concatenate([jnp.where(row > col, 1.0, 0.0), jnp.ones((Q, Q), F32)], axis=1).astype(BF16)
    uo2 = jnp.concatenate([uo, uo], axis=0)
    lower = lane < SB_DH
    srow = lax.broadcasted_iota(jnp.int32, (nch * Q, Q), 0)
    scol = lax.broadcasted_iota(jnp.int32, (nch * Q, Q), 1)

    qs = []
    for p in range(SB_PAIRS):
        q2 = q_ref[0, :, p * LANES:(p + 1) * LANES] * (SB_DH ** -0.5)
        zero = jnp.zeros_like(q2)
        qs.append(jnp.where(lower, q2, zero))
        qs.append(jnp.where(lower, zero, q2))

    def block(j, diag):
        rows = pl.ds(pl.multiple_of(j * Q, Q), Q)
        kj = k_ref[0, rows, :]
        vj = v_ref[0, rows, :]
        z = jnp.concatenate([_dot_nt(qs[c], kj[:, (c // 2) * LANES:(c // 2 + 1) * LANES])
                             for c in range(nch)], axis=0)
        ls = _log_sigmoid(z)
        stay = ls - z
        if diag:
            mask = scol < (srow & (Q - 1))
            stay = jnp.where(mask, stay, 0.0)
        hi = stay.astype(BF16)
        lo = (stay - hi.astype(F32)).astype(BF16)
        sums = _dot(jnp.concatenate([hi, lo], axis=1), uo2)
        if diag:
            w = jnp.where(mask, jnp.exp(ls + sums[:, :Q]), 0.0)
            car = sums[:, Q:]
        else:
            w = jnp.exp(ls + sums[:, :Q] + car_ref[...])
            car = car_ref[...] + sums[:, Q:]
        car_ref[...] = car
        wb = w.astype(BF16)
        pv = jnp.concatenate([_dot(wb[c * Q:(c + 1) * Q], vj[:, (c // 2) * LANES:(c // 2 + 1) * LANES])
                              for c in range(nch)], axis=0)
        if diag:
            acc_ref[...] = pv
        else:
            acc_ref[...] += pv
        return jnp.max(car)

    worst0 = block(i, True)

    def cond(s):
        j, worst = s
        return (j >= 0) & (worst > SB_EXIT)

    def body(s):
        j, _ = s
        return j - 1, block(j, False)

    lax.while_loop(cond, body, (i - 1, worst0))

    for p in range(SB_PAIRS):
        even = acc_ref[2 * p * Q:(2 * p + 1) * Q, :]
        odd = acc_ref[(2 * p + 1) * Q:(2 * p + 2) * Q, :]
        o_ref[0, :, p * LANES:(p + 1) * LANES] = jnp.where(lower, even, odd).astype(BF16)


def _sb(proj3):
    B, S, _ = proj3.shape
    W = SB_PAIRS * LANES
    base = (2 * GLA_HEADS * GLA_DK + 2 * GLA_HEADS * GLA_DV) // W
    nb = (SB_HEADS * SB_DH) // W
    return pl.pallas_call(
        _sb_kernel,
        grid=(B, nb, S // SB_BLOCK),
        in_specs=[
            pl.BlockSpec((1, SB_BLOCK, W), lambda b, p, i: (b, i, base + p)),
            pl.BlockSpec((1, S, W), lambda b, p, i: (b, 0, base + nb + p)),
            pl.BlockSpec((1, S, W), lambda b, p, i: (b, 0, base + 2 * nb + p)),
        ],
        out_specs=pl.BlockSpec((1, SB_BLOCK, W), lambda b, p, i: (b, i, p)),
        out_shape=jax.ShapeDtypeStruct((B, S, SB_HEADS * SB_DH), BF16),
        scratch_shapes=[
            pltpu.VMEM((2 * SB_PAIRS * SB_BLOCK, LANES), F32),
            pltpu.VMEM((2 * SB_PAIRS * SB_BLOCK, SB_BLOCK), F32),
        ],
        compiler_params=_params("parallel", "parallel", "arbitrary"),
        name="sb",
    )(proj3, proj3, proj3)


def _mix_kernel(gla_ref, sb_ref, ga_ref, gb_ref, x_ref, wa_ref, wb_ref, wo_ref, g_ref, h_ref, xt_ref):
    ya = _dot(gla_ref[...], wa_ref[...])
    yb = _dot(sb_ref[...], wb_ref[...])
    mixed = _sigmoid(ga_ref[...].astype(F32)) * ya + _sigmoid(gb_ref[...].astype(F32)) * yb
    h = x_ref[...] + _dot(mixed.astype(BF16), wo_ref[...])
    h_ref[...] = h
    xt_ref[...] = _rms(h, g_ref[...]).T.astype(BF16)


def _mix(gla_o, sb_o, proj, x2, wa, wb, wo, g):
    T = x2.shape[0]
    tm = min(512, T)
    D = D_MODEL
    gate_blk = (proj.shape[1] - 2 * D) // D
    full = lambda i: (0, 0)
    return pl.pallas_call(
        _mix_kernel,
        grid=(T // tm,),
        in_specs=[
            pl.BlockSpec((tm, D), lambda i: (i, 0)),
            pl.BlockSpec((tm, D), lambda i: (i, 0)),
            pl.BlockSpec((tm, D), lambda i: (i, gate_blk)),
            pl.BlockSpec((tm, D), lambda i: (i, gate_blk + 1)),
            pl.BlockSpec((tm, D), lambda i: (i, 0)),
            pl.BlockSpec((D, D), full),
            pl.BlockSpec((D, D), full),
            pl.BlockSpec((D, D), full),
            pl.BlockSpec((1, D), full),
        ],
        out_specs=[
            pl.BlockSpec((tm, D), lambda i: (i, 0)),
            pl.BlockSpec((D, tm), lambda i: (0, i)),
        ],
        out_shape=[
            jax.ShapeDtypeStruct((T, D), F32),
            jax.ShapeDtypeStruct((D, T), BF16),
        ],
        compiler_params=_params("parallel"),
        name="mix",
    )(gla_o, sb_o, proj, proj, x2, wa, wb, wo, g)


def _sort16_network():
    pairs = []

    def merge(lo, n, r):
        step = 2 * r
        if step < n:
            merge(lo, n, step)
            merge(lo + r, n, step)
            pairs.extend((i, i + r) for i in range(lo + r, lo + n - r, step))
        else:
            pairs.append((lo, lo + r))

    def sort(lo, n):
        if n > 1:
            sort(lo, n // 2)
            sort(lo + n // 2, n // 2)
            merge(lo, n, 1)

    sort(0, 16)
    return tuple(pairs)


_SORT16 = _sort16_network()
_BITONIC16 = tuple((i, i + d) for d in (8, 4, 2, 1) for i in range(16) if not i & d)


def _exchange(x, pairs):
    for i, j in pairs:
        x[i], x[j] = jnp.maximum(x[i], x[j]), jnp.minimum(x[i], x[j])


def _merge_sublanes(x):
    for shift in (4, 2, 1):
        y = [pltpu.roll(v, shift, 0) for v in x]
        x = [jnp.maximum(x[i], y[15 - i]) for i in range(16)]
        _exchange(x, _BITONIC16)
    return x


def _route_kernel(xt_ref, wq_ref, k1_ref, k2_ref, cnt_ref, e1_ref, rk_ref, e2_ref, q_scr):
    K = PEER_TOPK
    tt = xt_ref.shape[1]
    sub = lax.broadcasted_iota(jnp.int32, (8, tt), 0)
    q_scr[...] = _dot(wq_ref[...], xt_ref[...]).astype(BF16)

    def head(h, _):
        qt = q_scr[pl.ds(pl.multiple_of(h * 2 * PEER_HALF, 2 * PEER_HALF), 2 * PEER_HALF), :]
        s1 = _dot(k1_ref[h], qt[:PEER_HALF])
        s2 = _dot(k2_ref[h], qt[PEER_HALF:])
        s1 = [s1[8 * k:8 * k + 8] for k in range(16)]
        s2 = [s2[8 * k:8 * k + 8] for k in range(16)]
        v1, v2 = list(s1), list(s2)
        _exchange(v1, _SORT16)
        _exchange(v2, _SORT16)
        v1 = _merge_sublanes(v1)
        v2 = _merge_sublanes(v2)

        lo8, hi8 = v1[0], v1[8]
        for a in range(1, 8):
            lo8 = jnp.where(sub == a, v1[a], lo8)
            hi8 = jnp.where(sub == a, v1[8 + a], hi8)
        p = [lo8 + v2[0]] + [jnp.where(sub < K // (b + 1), lo8 + v2[b], NEG_INF) for b in range(1, K)]
        ph = hi8 + v2[0]
        top = list(p)
        top[K - 1] = jnp.maximum(top[K - 1], ph)
        _exchange(top, _BITONIC16)
        top = _merge_sublanes(top)
        thr = top[K - 1]
        z = jnp.ones_like(thr)
        for i in range(1, K):
            z = z + jnp.exp(top[i] - top[0])
        inv_z = 1.0 / z

        pairs_lo = jnp.where(p[0] >= thr, 1.0, 0.0)
        for b in range(1, K):
            pairs_lo = pairs_lo + jnp.where(p[b] >= thr, 1.0, 0.0)
        pairs_hi = jnp.where(ph >= thr, 1.0, 0.0)
        pairs = [jnp.broadcast_to((pairs_lo if a < 8 else pairs_hi)[a % 8:a % 8 + 1], (8, tt)) for a in range(K)]

        for k in range(0, 16, 2):
            cnt, rk = [], []
            for kk in (k, k + 1):
                c = jnp.zeros((8, tt), F32)
                r = jnp.zeros((8, tt), F32)
                for a in range(K):
                    c = jnp.where(s1[kk] == v1[a], pairs[a], c)
                    r = jnp.where(v2[a] > s2[kk], a + 1.0, r)
                cnt.append(c)
                rk.append(r)
            rows = slice(8 * k, 8 * k + 16)
            cnt_ref[h, rows, :] = jnp.concatenate(cnt, axis=0)
            rk_ref[h, rows, :] = jnp.concatenate(rk, axis=0).astype(BF16)
            e1_ref[h, rows, :] = jnp.concatenate([jnp.exp(s1[k] - v1[0]) * inv_z,
                                                  jnp.exp(s1[k + 1] - v1[0]) * inv_z], axis=0)
            e2_ref[h, rows, :] = jnp.concatenate([jnp.exp(s2[k] - v2[0]),
                                                  jnp.exp(s2[k + 1] - v2[0])], axis=0).astype(BF16)
        return 0

    lax.fori_loop(0, PEER_HEADS, head, 0)


def _route(xt, wq_t, k1, k2):
    D, T = xt.shape
    H, N = PEER_HEADS, PEER_NKEYS
    tt = min(512, T)
    return pl.pallas_call(
        _route_kernel,
        grid=(T // tt,),
        in_specs=[
            pl.BlockSpec((D, tt), lambda i: (0, i)),
            pl.BlockSpec(wq_t.shape, lambda i: (0, 0)),
            pl.BlockSpec(k1.shape, lambda i: (0, 0, 0)),
            pl.BlockSpec(k2.shape, lambda i: (0, 0, 0)),
        ],
        out_specs=[pl.BlockSpec((H, N, tt), lambda i: (0, 0, i))] * 4,
        out_shape=[
            jax.ShapeDtypeStruct((H, N, T), F32),
            jax.ShapeDtypeStruct((H, N, T), F32),
            jax.ShapeDtypeStruct((H, N, T), BF16),
            jax.ShapeDtypeStruct((H, N, T), BF16),
        ],
        scratch_shapes=[pltpu.VMEM((wq_t.shape[0], tt), BF16)],
        compiler_params=_params("parallel"),
        name="route",
    )(xt, wq_t, k1, k2)


PEER_NI = 4
PEER_TT = 1024


def _peer_kernel(xt_ref, u_ref, vt_ref, cnt_cur, e1_cur, cnt_prev, e1_prev, rk_ref, e2_ref, gk_ref, o_ref,
                 act0, act1, a0, a1):
    s = pl.program_id(1)
    N = PEER_NKEYS
    te = PEER_NI * N
    k3 = gk_ref[0:1, 0:1].astype(BF16)
    k1 = gk_ref[0:1, 1:2].astype(BF16)

    def scores(half, act_ref):
        act_ref[...] = _dot(u_ref[half * te:(half + 1) * te, :], xt_ref[...])

    def gates(cnt_ref, e1_ref, half, n, act_ref, a_ref):
        W = 2 * LANES
        for lg in range(xt_ref.shape[1] // W):
            ls = slice(lg * W, (lg + 1) * W)
            row = slice(half * PEER_NI + n, half * PEER_NI + n + 1)
            g = jnp.zeros((N, W), BF16)
            for h in range(PEER_HEADS):
                cb = jnp.broadcast_to(cnt_ref[h, row, ls], g.shape).astype(BF16)
                eb = jnp.broadcast_to(e1_ref[h, row, ls], g.shape).astype(BF16)
                g = g + jnp.where(rk_ref[h, :, ls] < cb, e2_ref[h, :, ls], jnp.zeros((), BF16)) * eb
            a_ref[n * N:(n + 1) * N, ls] = _gelu_bf16(act_ref[n * N:(n + 1) * N, ls].astype(BF16), k3, k1) * g

    def mix(half, a_ref):
        o_ref[...] += _dot(vt_ref[:, half * te:(half + 1) * te], a_ref[...])

    last = pl.num_programs(1) - 1

    @pl.when(s == 0)
    def _():
        o_ref[...] = jnp.zeros_like(o_ref)
        scores(0, act0)
        scores(1, act1)
        for n in range(PEER_NI):
            gates(cnt_cur, e1_cur, 0, n, act0, a0)

    @pl.when((s > 0) & (s < last))
    def _():
        scores(0, act0)
        gates(cnt_prev, e1_prev, 1, 0, act1, a1)
        gates(cnt_prev, e1_prev, 1, 1, act1, a1)
        mix(0, a0)
        gates(cnt_prev, e1_prev, 1, 2, act1, a1)
        gates(cnt_prev, e1_prev, 1, 3, act1, a1)

        scores(1, act1)
        gates(cnt_cur, e1_cur, 0, 0, act0, a0)
        gates(cnt_cur, e1_cur, 0, 1, act0, a0)
        mix(1, a1)
        gates(cnt_cur, e1_cur, 0, 2, act0, a0)
        gates(cnt_cur, e1_cur, 0, 3, act0, a0)

    @pl.when(s == last)
    def _():
        gates(cnt_prev, e1_prev, 1, 0, act1, a1)
        gates(cnt_prev, e1_prev, 1, 1, act1, a1)
        mix(0, a0)
        gates(cnt_prev, e1_prev, 1, 2, act1, a1)
        gates(cnt_prev, e1_prev, 1, 3, act1, a1)
        mix(1, a1)


def _peer(xt, u_bf, vt_bf, cnt, e1, rk, e2):
    D, T = xt.shape
    H, N = PEER_HEADS, PEER_NKEYS
    tt = min(PEER_TT, T)
    te = PEER_NI * N
    nblk = (N * N) // (2 * te)
    cur = lambda i, s: (0, jnp.minimum(s, nblk - 1), i)
    prev = lambda i, s: (0, jnp.maximum(s - 1, 0), i)
    return pl.pallas_call(
        _peer_kernel,
        grid=(T // tt, nblk + 1),
        in_specs=[
            pl.BlockSpec((D, tt), lambda i, s: (0, i)),
            pl.BlockSpec((2 * te, D), lambda i, s: (jnp.minimum(s, nblk - 1), 0)),
            pl.BlockSpec((D, 2 * te), lambda i, s: (0, jnp.maximum(s - 1, 0))),
            pl.BlockSpec((H, 2 * PEER_NI, tt), cur),
            pl.BlockSpec((H, 2 * PEER_NI, tt), cur),
            pl.BlockSpec((H, 2 * PEER_NI, tt), prev),
            pl.BlockSpec((H, 2 * PEER_NI, tt), prev),
            pl.BlockSpec((H, N, tt), lambda i, s: (0, 0, i)),
            pl.BlockSpec((H, N, tt), lambda i, s: (0, 0, i)),
            pl.BlockSpec((8, LANES), lambda i, s: (0, 0)),
        ],
        out_specs=pl.BlockSpec((D, tt), lambda i, s: (0, i)),
        out_shape=jax.ShapeDtypeStruct((D, T), F32),
        scratch_shapes=[pltpu.VMEM((te, tt), F32)] * 2 + [pltpu.VMEM((te, tt), BF16)] * 2,
        compiler_params=_params("parallel", "arbitrary"),
        name="peer",
    )(xt, u_bf, vt_bf, cnt, e1, cnt, e1, rk, e2, _gelu_consts())


def _vprep_kernel(v_ref, o_ref):
    o_ref[...] = v_ref[...].T.astype(BF16)


def _vprep(v):
    E, D = v.shape
    te = 2048
    return pl.pallas_call(
        _vprep_kernel,
        grid=(E // te,),
        in_specs=[pl.BlockSpec((te, D), lambda i: (i, 0))],
        out_specs=pl.BlockSpec((D, te), lambda i: (0, i)),
        out_shape=jax.ShapeDtypeStruct((D, E), BF16),
        compiler_params=_params("parallel"),
        name="vprep",
    )(v)


def _final_kernel(h_ref, pt_ref, p_ref, wp_ref, wg_ref, gp_ref, gf_ref, o_ref):
    h2 = h_ref[...] + pt_ref[...].T
    ple = _dot(p_ref[...].astype(BF16), wp_ref[...])
    gate = _sigmoid(_dot(_rms(h2, gp_ref[...]).astype(BF16), wg_ref[...]))
    o_ref[...] = _rms(h2 + ple * gate, gf_ref[...])


def _final(h1, peer_t, p2, wp, wg, gp, gf):
    T, D = h1.shape
    tm = min(512, T)
    full = lambda i: (0, 0)
    return pl.pallas_call(
        _final_kernel,
        grid=(T // tm,),
        in_specs=[
            pl.BlockSpec((tm, D), lambda i: (i, 0)),
            pl.BlockSpec((D, tm), lambda i: (0, i)),
            pl.BlockSpec((tm, PLE_DIM), lambda i: (i, 0)),
            pl.BlockSpec((PLE_DIM, D), full),
            pl.BlockSpec((D, D), full),
            pl.BlockSpec((1, D), full),
            pl.BlockSpec((1, D), full),
        ],
        out_specs=pl.BlockSpec((tm, D), lambda i: (i, 0)),
        out_shape=jax.ShapeDtypeStruct((T, D), F32),
        compiler_params=_params("parallel"),
        name="final",
    )(h1, peer_t, p2, wp, wg, gp, gf)


def kernel(x, p, g_mix, w_in, w_gla_a_up, b_gla_a, g_gla_o, w_gla_out, w_sb_out, w_o, g_ffn, w_peer_q, peer_k1, peer_k2, peer_u, peer_v, g_ple, w_ple_gate, w_ple, g_final):
    B, S, D = x.shape
    T = B * S
    depth = w_in.shape[0]
    assert depth == 1, "the final RMSNorm is fused into the last stage of a single layer"
    qk = GLA_HEADS * GLA_DK
    gv = GLA_HEADS * GLA_DV
    lo = 2 * qk + 2 * gv
    h = x.reshape(T, D)
    for i in range(depth):
        w_main = jnp.concatenate([w_in[i][:, :lo], w_in[i][:, lo + GLA_RANK:]], axis=1).astype(BF16)
        w_ga = jnp.pad(w_in[i][:, lo:lo + GLA_RANK], ((0, 0), (0, LANES - GLA_RANK))).astype(BF16)
        wup = jnp.pad(w_gla_a_up[i], ((0, LANES - GLA_RANK), (0, 0)))

        proj, a_lo = _inproj(h, g_mix[i][None], w_main, w_ga)
        proj3 = proj.reshape(B, S, -1)
        gla_o = _gla(proj3, a_lo.reshape(B, S, LANES), wup, b_gla_a[i][None], g_gla_o[i][None])
        sb_o = _sb(proj3)
        h1, xt = _mix(gla_o.reshape(T, -1), sb_o.reshape(T, -1), proj, h,
                      w_gla_out[i].astype(BF16), w_sb_out[i].astype(BF16), w_o[i].astype(BF16), g_ffn[i][None])
        cnt, e1, rk, e2 = _route(xt, w_peer_q[i].astype(BF16).T, peer_k1[i].astype(BF16), peer_k2[i].astype(BF16))
        peer_t = _peer(xt, peer_u[i].astype(BF16), _vprep(peer_v[i]), cnt, e1, rk, e2)
        h = _final(h1, peer_t, p[i].reshape(T, -1), w_ple[i].astype(BF16), w_ple_gate[i].astype(BF16),
                   g_ple[i][None], g_final[None])
    return h.reshape(B, S, D)
```

```python
import functools
import math

import jax
import jax.numpy as jnp
from jax import lax
from jax.experimental import pallas as pl
from jax.experimental.pallas import tpu as pltpu

F32 = jnp.float32
BF16 = jnp.bfloat16

D_MODEL = 1024
EPS = 1e-6
CHUNK = 64
GLA_HEADS, GLA_DK, GLA_DV, GLA_RANK, GLA_TAU = 4, 128, 256, 16, 16.0
SB_HEADS, SB_DH, SB_BLOCK = 16, 64, 128
PEER_HEADS, PEER_NKEYS, PEER_TOPK = 8, 128, 16
PEER_HALF = 128
PLE_DIM = 256

LANES = 128
VMEM_LIMIT = 56 * 1024 * 1024

SB_EXIT = -64.0

GLA_GROUP = 4 * CHUNK

NEG_INF = float("-inf")


def _params(*sem):
    return pltpu.CompilerParams(dimension_semantics=sem, vmem_limit_bytes=VMEM_LIMIT)


def _rms(x, g):
    return x * lax.rsqrt(jnp.mean(x * x, axis=-1, keepdims=True) + EPS) * g


def _sigmoid(x):
    return 1.0 / (1.0 + jnp.exp(-x))


def _log_sigmoid(x):
    return jnp.minimum(x, 0.0) - jnp.log(1.0 + jnp.exp(-jnp.abs(x)))


def _gelu_tanh(x):
    c = math.sqrt(2.0 / math.pi)
    return 0.5 * x * (1.0 + jnp.tanh(c * (x + 0.044715 * (x * x * x))))


def _gelu_consts():
    k = -2.0 * math.sqrt(2.0 / math.pi) * math.log2(math.e)
    return jnp.zeros((8, LANES), F32).at[0, 0].set(k * 0.044715).at[0, 1].set(k)


def _gelu_bf16(x, k3, k1):
    t = x * (k3 * (x * x) + k1)
    return x * (1.0 / (1.0 + jnp.exp2(t)))


def _dot(a, b):
    return jnp.dot(a, b, preferred_element_type=F32)


def _dot_nt(a, b):
    return lax.dot_general(a, b, (((1,), (1,)), ((), ())), preferred_element_type=F32)


def _dot_tn(a, b):
    return lax.dot_general(a, b, (((0,), (0,)), ((), ())), preferred_element_type=F32)


def _inproj_kernel(x_ref, g_ref, w_ref, wga_ref, o_ref, ga_ref, u_scr):
    @pl.when(pl.program_id(1) == 0)
    def _():
        u_scr[...] = _rms(x_ref[...], g_ref[...]).astype(BF16)
        ga_ref[...] = _dot(u_scr[...], wga_ref[...])

    o_ref[...] = _dot(u_scr[...], w_ref[...]).astype(BF16)


def _inproj(x2, g, w_main, w_ga):
    T = x2.shape[0]
    N = w_main.shape[1]
    tm = min(1024, T)
    tn = 4096
    return pl.pallas_call(
        _inproj_kernel,
        grid=(T // tm, N // tn),
        in_specs=[
            pl.BlockSpec((tm, D_MODEL), lambda i, j: (i, 0)),
            pl.BlockSpec((1, D_MODEL), lambda i, j: (0, 0)),
            pl.BlockSpec((D_MODEL, tn), lambda i, j: (0, j)),
            pl.BlockSpec((D_MODEL, LANES), lambda i, j: (0, 0)),
        ],
        out_specs=[
            pl.BlockSpec((tm, tn), lambda i, j: (i, j)),
            pl.BlockSpec((tm, LANES), lambda i, j: (i, 0)),
        ],
        out_shape=[
            jax.ShapeDtypeStruct((T, N), BF16),
            jax.ShapeDtypeStruct((T, LANES), F32),
        ],
        scratch_shapes=[pltpu.VMEM((tm, D_MODEL), BF16)],
        compiler_params=_params("parallel", "arbitrary"),
        name="inproj",
    )(x2, g, w_main, w_ga)


def _gla_kernel(q_ref, k_ref, v_ref, r_ref, ga_ref, wup_ref, ba_ref, go_ref, o_ref, kd_scr, dec_scr, st_scr):
    S = q_ref.shape[1]
    G = GLA_GROUP
    row = lax.broadcasted_iota(jnp.int32, (G, G), 0)
    col = lax.broadcasted_iota(jnp.int32, (G, G), 1)
    same = (row // CHUNK) == (col // CHUNK)
    cm = jnp.concatenate([jnp.where(same & (col <= row), 1.0, 0.0), jnp.where(same, 1.0, 0.0)], axis=0).astype(BF16)

    ga = ga_ref[0]
    ga_hi = ga.astype(BF16)
    ga_lo = (ga - ga_hi.astype(F32)).astype(BF16)
    wu = wup_ref[...]
    wu_hi = wu.astype(BF16)
    wu_lo = (wu - wu_hi.astype(F32)).astype(BF16)
    pre = (_dot(jnp.concatenate([ga_hi, ga_lo], axis=1), jnp.concatenate([wu_hi, wu_hi], axis=0))
           + _dot(ga_hi, wu_lo) + ba_ref[...])
    log_a = _log_sigmoid(pre) * (1.0 / GLA_TAU)
    hi = log_a.astype(BF16)
    lo = (log_a - hi.astype(F32)).astype(BF16)
    hl = jnp.concatenate([hi, lo], axis=1)
    for g in range(S // G):
        rows = slice(g * G, (g + 1) * G)
        sums = _dot(cm, hl[rows])
        sums = sums[:, :GLA_DK] + sums[:, GLA_DK:]
        logb, tot = sums[:G], sums[G:]
        kd_scr[rows, :] = (k_ref[0, rows, :].astype(F32) * jnp.exp(tot - logb)).astype(BF16)
        dec_scr[rows, :] = jnp.exp(tot)

    WG = 2 * G
    per = WG // CHUNK

    def walk(gi, st):
        base = pl.multiple_of(gi * WG, WG)
        kv = [_dot_tn(v_ref[0, pl.ds(base + c * CHUNK, CHUNK), :], kd_scr[pl.ds(base + c * CHUNK, CHUNK), :])
              for c in range(per)]
        for c in range(per):
            st = dec_scr[pl.ds(base + c * CHUNK, 1), :] * st + kv[c]
            st_scr[gi * per + c] = st.astype(BF16)
        return st

    lax.fori_loop(0, S // WG, walk, jnp.zeros((GLA_DV, GLA_DK), F32))

    def emit(gi, _):
        base = pl.multiple_of(gi * WG, WG)
        o = jnp.concatenate([_dot_nt(q_ref[0, pl.ds(base + c * CHUNK, CHUNK), :], st_scr[gi * per + c])
                             for c in range(per)], axis=0) * (GLA_DK ** -0.5)
        o = _rms(o, go_ref[...])
        rows = pl.ds(base, WG)
        r = r_ref[0, rows, :].astype(F32)
        o_ref[0, rows, :] = (o * (r * _sigmoid(r))).astype(BF16)
        return 0

    lax.fori_loop(0, S // WG, emit, 0)


def _gla(proj3, ga3, wup, ba, go):
    B, S, _ = proj3.shape
    H = GLA_HEADS
    kb = (H * GLA_DK) // GLA_DK
    vb = (2 * H * GLA_DK) // GLA_DV
    rb = vb + H
    return pl.pallas_call(
        _gla_kernel,
        grid=(B, H),
        in_specs=[
            pl.BlockSpec((1, S, GLA_DK), lambda b, h: (b, 0, h)),
            pl.BlockSpec((1, S, GLA_DK), lambda b, h: (b, 0, kb + h)),
            pl.BlockSpec((1, S, GLA_DV), lambda b, h: (b, 0, vb + h)),
            pl.BlockSpec((1, S, GLA_DV), lambda b, h: (b, 0, rb + h)),
            pl.BlockSpec((1, S, LANES), lambda b, h: (b, 0, 0)),
            pl.BlockSpec((LANES, GLA_DK), lambda b, h: (0, h)),
            pl.BlockSpec((1, GLA_DK), lambda b, h: (0, h)),
            pl.BlockSpec((1, GLA_DV), lambda b, h: (0, h)),
        ],
        out_specs=pl.BlockSpec((1, S, GLA_DV), lambda b, h: (b, 0, h)),
        out_shape=jax.ShapeDtypeStruct((B, S, H * GLA_DV), BF16),
        scratch_shapes=[
            pltpu.VMEM((S, GLA_DK), BF16),
            pltpu.VMEM((S, GLA_DK), F32),
            pltpu.VMEM((S // CHUNK, GLA_DV, GLA_DK), BF16),
        ],
        compiler_params=_params("parallel", "parallel"),
        name="gla",
    )(proj3, proj3, proj3, proj3, ga3, wup, ba, go)


SB_PAIRS = 8


def _sb_kernel(q_ref, k_ref, v_ref, o_ref, acc_ref, car_ref):
    i = pl.program_id(2)
    Q = SB_BLOCK
    nch = 2 * SB_PAIRS
    row = lax.broadcasted_iota(jnp.int32, (Q, Q), 0)
    col = lax.broadcasted_iota(jnp.int32, (Q, Q), 1)
    lane = lax.broadcasted_iota(jnp.int32, (Q, LANES), 1)
    uo = jnp.concatenate([jnp.where(row > col, 1.0, 0.0), jnp.ones((Q, Q), F32)], axis=1).astype(BF16)
    uo2 = jnp.concatenate([uo, uo], axis=0)
    lower = lane < SB_DH
    srow = lax.broadcasted_iota(jnp.int32, (nch * Q, Q), 0)
    scol = lax.broadcasted_iota(jnp.int32, (nch * Q, Q), 1)

    qs = []
    for p in range(SB_PAIRS):
        q2 = q_ref[0, :, p * LANES:(p + 1) * LANES] * (SB_DH ** -0.5)
        zero = jnp.zeros_like(q2)
        qs.append(jnp.where(lower, q2, zero))
        qs.append(jnp.where(lower, zero, q2))

    def block(j, diag):
        rows = pl.ds(pl.multiple_of(j * Q, Q), Q)
        kj = k_ref[0, rows, :]
        vj = v_ref[0, rows, :]
        z = jnp.concatenate([_dot_nt(qs[c], kj[:, (c // 2) * LANES:(c // 2 + 1) * LANES])
                             for c in range(nch)], axis=0)
        ls = _log_sigmoid(z)
        stay = ls - z
        if diag:
            mask = scol < (srow & (Q - 1))
            stay = jnp.where(mask, stay, 0.0)
        hi = stay.astype(BF16)
        lo = (stay - hi.astype(F32)).astype(BF16)
        sums = _dot(jnp.concatenate([hi, lo], axis=1), uo2)
        if diag:
            w = jnp.where(mask, jnp.exp(ls + sums[:, :Q]), 0.0)
            car = sums[:, Q:]
        else:
            w = jnp.exp(ls + sums[:, :Q] + car_ref[...])
            car = car_ref[...] + sums[:, Q:]
        car_ref[...] = car
        wb = w.astype(BF16)
        pv = jnp.concatenate([_dot(wb[c * Q:(c + 1) * Q], vj[:, (c // 2) * LANES:(c // 2 + 1) * LANES])
                              for c in range(nch)], axis=0)
        if diag:
            acc_ref[...] = pv
        else:
            acc_ref[...] += pv
        return jnp.max(car)

    worst0 = block(i, True)

    def cond(s):
        j, worst = s
        return (j >= 0) & (worst > SB_EXIT)

    def body(s):
        j, _ = s
        return j - 1, block(j, False)

    lax.while_loop(cond, body, (i - 1, worst0))

    for p in range(SB_PAIRS):
        even = acc_ref[2 * p * Q:(2 * p + 1) * Q, :]
        odd = acc_ref[(2 * p + 1) * Q:(2 * p + 2) * Q, :]
        o_ref[0, :, p * LANES:(p + 1) * LANES] = jnp.where(lower, even, odd).astype(BF16)


def _sb(proj3):
    B, S, _ = proj3.shape
    W = SB_PAIRS * LANES
    base = (2 * GLA_HEADS * GLA_DK + 2 * GLA_HEADS * GLA_DV) // W
    nb = (SB_HEADS * SB_DH) // W
    return pl.pallas_call(
        _sb_kernel,
        grid=(B, nb, S // SB_BLOCK),
        in_specs=[
            pl.BlockSpec((1, SB_BLOCK, W), lambda b, p, i: (b, i, base + p)),
            pl.BlockSpec((1, S, W), lambda b, p, i: (b, 0, base + nb + p)),
            pl.BlockSpec((1, S, W), lambda b, p, i: (b, 0, base + 2 * nb + p)),
        ],
        out_specs=pl.BlockSpec((1, SB_BLOCK, W), lambda b, p, i: (b, i, p)),
        out_shape=jax.ShapeDtypeStruct((B, S, SB_HEADS * SB_DH), BF16),
        scratch_shapes=[
            pltpu.VMEM((2 * SB_PAIRS * SB_BLOCK, LANES), F32),
            pltpu.VMEM((2 * SB_PAIRS * SB_BLOCK, SB_BLOCK), F32),
        ],
        compiler_params=_params("parallel", "parallel", "arbitrary"),
        name="sb",
    )(proj3, proj3, proj3)


def _mix_kernel(gla_ref, sb_ref, ga_ref, gb_ref, x_ref, wa_ref, wb_ref, wo_ref, g_ref, h_ref, xt_ref):
    ya = _dot(gla_ref[...], wa_ref[...])
    yb = _dot(sb_ref[...], wb_ref[...])
    mixed = _sigmoid(ga_ref[...].astype(F32)) * ya + _sigmoid(gb_ref[...].astype(F32)) * yb
    h = x_ref[...] + _dot(mixed.astype(BF16), wo_ref[...])
    h_ref[...] = h
    xt_ref[...] = _rms(h, g_ref[...]).T.astype(BF16)


def _mix(gla_o, sb_o, proj, x2, wa, wb, wo, g):
    T = x2.shape[0]
    tm = min(512, T)
    D = D_MODEL
    gate_blk = (proj.shape[1] - 2 * D) // D
    full = lambda i: (0, 0)
    return pl.pallas_call(
        _mix_kernel,
        grid=(T // tm,),
        in_specs=[
            pl.BlockSpec((tm, D), lambda i: (i, 0)),
            pl.BlockSpec((tm, D), lambda i: (i, 0)),
            pl.BlockSpec((tm, D), lambda i: (i, gate_blk)),
            pl.BlockSpec((tm, D), lambda i: (i, gate_blk + 1)),
            pl.BlockSpec((tm, D), lambda i: (i, 0)),
            pl.BlockSpec((D, D), full),
            pl.BlockSpec((D, D), full),
            pl.BlockSpec((D, D), full),
            pl.BlockSpec((1, D), full),
        ],
        out_specs=[
            pl.BlockSpec((tm, D), lambda i: (i, 0)),
            pl.BlockSpec((D, tm), lambda i: (0, i)),
        ],
        out_shape=[
            jax.ShapeDtypeStruct((T, D), F32),
            jax.ShapeDtypeStruct((D, T), BF16),
        ],
        compiler_params=_params("parallel"),
        name="mix",
    )(gla_o, sb_o, proj, proj, x2, wa, wb, wo, g)


def _sort16_network():
    pairs = []

    def merge(lo, n, r):
        step = 2 * r
        if step < n:
            merge(lo, n, step)
            merge(lo + r, n, step)
            pairs.extend((i, i + r) for i in range(lo + r, lo + n - r, step))
        else:
            pairs.append((lo, lo + r))

    def sort(lo, n):
        if n > 1:
            sort(lo, n // 2)
            sort(lo + n // 2, n // 2)
            merge(lo, n, 1)

    sort(0, 16)
    return tuple(pairs)


_SORT16 = _sort16_network()
_BITONIC16 = tuple((i, i + d) for d in (8, 4, 2, 1) for i in range(16) if not i & d)


def _exchange(x, pairs):
    for i, j in pairs:
        x[i], x[j] = jnp.maximum(x[i], x[j]), jnp.minimum(x[i], x[j])


def _merge_sublanes(x):
    for shift in (4, 2, 1):
        y = [pltpu.roll(v, shift, 0) for v in x]
        x = [jnp.maximum(x[i], y[15 - i]) for i in range(16)]
        _exchange(x, _BITONIC16)
    return x


def _route_kernel(xt_ref, wq_ref, k1_ref, k2_ref, cnt_ref, e1_ref, rk_ref, e2_ref, q_scr):
    K = PEER_TOPK
    tt = xt_ref.shape[1]
    sub = lax.broadcasted_iota(jnp.int32, (8, tt), 0)
    q_scr[...] = _dot(wq_ref[...], xt_ref[...]).astype(BF16)

    def head(h, _):
        qt = q_scr[pl.ds(pl.multiple_of(h * 2 * PEER_HALF, 2 * PEER_HALF), 2 * PEER_HALF), :]
        s1 = _dot(k1_ref[h], qt[:PEER_HALF])
        s2 = _dot(k2_ref[h], qt[PEER_HALF:])
        s1 = [s1[8 * k:8 * k + 8] for k in range(16)]
        s2 = [s2[8 * k:8 * k + 8] for k in range(16)]
        v1, v2 = list(s1), list(s2)
        _exchange(v1, _SORT16)
        _exchange(v2, _SORT16)
        v1 = _merge_sublanes(v1)
        v2 = _merge_sublanes(v2)

        lo8, hi8 = v1[0], v1[8]
        for a in range(1, 8):
            lo8 = jnp.where(sub == a, v1[a], lo8)
            hi8 = jnp.where(sub == a, v1[8 + a], hi8)
        p = [lo8 + v2[0]] + [jnp.where(sub < K // (b + 1), lo8 + v2[b], NEG_INF) for b in range(1, K)]
        ph = hi8 + v2[0]
        top = list(p)
        top[K - 1] = jnp.maximum(top[K - 1], ph)
        _exchange(top, _BITONIC16)
        top = _merge_sublanes(top)
        thr = top[K - 1]
        z = jnp.ones_like(thr)
        for i in range(1, K):
            z = z + jnp.exp(top[i] - top[0])
        inv_z = 1.0 / z

        pairs_lo = jnp.where(p[0] >= thr, 1.0, 0.0)
        for b in range(1, K):
            pairs_lo = pairs_lo + jnp.where(p[b] >= thr, 1.0, 0.0)
        pairs_hi = jnp.where(ph >= thr, 1.0, 0.0)
        pairs = [jnp.broadcast_to((pairs_lo if a < 8 else pairs_hi)[a % 8:a % 8 + 1], (8, tt)) for a in range(K)]

        for k in range(0, 16, 2):
            cnt, rk = [], []
            for kk in (k, k + 1):
                c = jnp.zeros((8, tt), F32)
                r = jnp.zeros((8, tt), F32)
                for a in range(K):
                    c = jnp.where(s1[kk] == v1[a], pairs[a], c)
                    r = jnp.where(v2[a] > s2[kk], a + 1.0, r)
                cnt.append(c)
                rk.append(r)
            rows = slice(8 * k, 8 * k + 16)
            cnt_ref[h, rows, :] = jnp.concatenate(cnt, axis=0)
            rk_ref[h, rows, :] = jnp.concatenate(rk, axis=0).astype(BF16)
            e1_ref[h, rows, :] = jnp.concatenate([jnp.exp(s1[k] - v1[0]) * inv_z,
                                                  jnp.exp(s1[k + 1] - v1[0]) * inv_z], axis=0)
            e2_ref[h, rows, :] = jnp.concatenate([jnp.exp(s2[k] - v2[0]),
                                                  jnp.exp(s2[k + 1] - v2[0])], axis=0).astype(BF16)
        return 0

    lax.fori_loop(0, PEER_HEADS, head, 0)


def _route(xt, wq_t, k1, k2):
    D, T = xt.shape
    H, N = PEER_HEADS, PEER_NKEYS
    tt = min(1024, T)
    return pl.pallas_call(
        _route_kernel,
        grid=(T // tt,),
        in_specs=[
            pl.BlockSpec((D, tt), lambda i: (0, i)),
            pl.BlockSpec(wq_t.shape, lambda i: (0, 0)),
            pl.BlockSpec(k1.shape, lambda i: (0, 0, 0)),
            pl.BlockSpec(k2.shape, lambda i: (0, 0, 0)),
        ],
        out_specs=[pl.BlockSpec((H, N, tt), lambda i: (0, 0, i))] * 4,
        out_shape=[
            jax.ShapeDtypeStruct((H, N, T), F32),
            jax.ShapeDtypeStruct((H, N, T), F32),
            jax.ShapeDtypeStruct((H, N, T), BF16),
            jax.ShapeDtypeStruct((H, N, T), BF16),
        ],
        scratch_shapes=[pltpu.VMEM((wq_t.shape[0], tt), BF16)],
        compiler_params=_params("parallel"),
        name="route",
    )(xt, wq_t, k1, k2)


PEER_NI = 4
PEER_TT = 1024


def _peer_kernel(xt_ref, u_ref, vt_ref, cnt_cur, e1_cur, cnt_prev, e1_prev, rk_ref, e2_ref, gk_ref, o_ref,
                 act0, act1, a0, a1):
    s = pl.program_id(1)
    N = PEER_NKEYS
    te = PEER_NI * N
    k3 = gk_ref[0:1, 0:1].astype(BF16)
    k1 = gk_ref[0:1, 1:2].astype(BF16)

    def scores(half, act_ref):
        act_ref[...] = _dot(u_ref[half * te:(half + 1) * te, :], xt_ref[...])

    def gates(cnt_ref, e1_ref, half, n, act_ref, a_ref):
        W = 2 * LANES
        for lg in range(xt_ref.shape[1] // W):
            ls = slice(lg * W, (lg + 1) * W)
            row = slice(half * PEER_NI + n, half * PEER_NI + n + 1)
            g = jnp.zeros((N, W), BF16)
            for h in range(PEER_HEADS):
                cb = jnp.broadcast_to(cnt_ref[h, row, ls], g.shape).astype(BF16)
                eb = jnp.broadcast_to(e1_ref[h, row, ls], g.shape).astype(BF16)
                g = g + jnp.where(rk_ref[h, :, ls] < cb, e2_ref[h, :, ls], jnp.zeros((), BF16)) * eb
            a_ref[n * N:(n + 1) * N, ls] = _gelu_bf16(act_ref[n * N:(n + 1) * N, ls].astype(BF16), k3, k1) * g

    def mix(half, a_ref):
        o_ref[...] += _dot(vt_ref[:, half * te:(half + 1) * te], a_ref[...])

    last = pl.num_programs(1) - 1

    @pl.when(s == 0)
    def _():
        o_ref[...] = jnp.zeros_like(o_ref)
        scores(0, act0)
        scores(1, act1)
        for n in range(PEER_NI):
            gates(cnt_cur, e1_cur, 0, n, act0, a0)

    @pl.when((s > 0) & (s < last))
    def _():
        scores(0, act0)
        gates(cnt_prev, e1_prev, 1, 0, act1, a1)
        gates(cnt_prev, e1_prev, 1, 1, act1, a1)
        mix(0, a0)
        gates(cnt_prev, e1_prev, 1, 2, act1, a1)
        gates(cnt_prev, e1_prev, 1, 3, act1, a1)

        scores(1, act1)
        gates(cnt_cur, e1_cur, 0, 0, act0, a0)
        gates(cnt_cur, e1_cur, 0, 1, act0, a0)
        mix(1, a1)
        gates(cnt_cur, e1_cur, 0, 2, act0, a0)
        gates(cnt_cur, e1_cur, 0, 3, act0, a0)

    @pl.when(s == last)
    def _():
        gates(cnt_prev, e1_prev, 1, 0, act1, a1)
        gates(cnt_prev, e1_prev, 1, 1, act1, a1)
        mix(0, a0)
        gates(cnt_prev, e1_prev, 1, 2, act1, a1)
        gates(cnt_prev, e1_prev, 1, 3, act1, a1)
        mix(1, a1)


def _peer(xt, u_bf, vt_bf, cnt, e1, rk, e2):
    D, T = xt.shape
    H, N = PEER_HEADS, PEER_NKEYS
    tt = min(PEER_TT, T)
    te = PEER_NI * N
    nblk = (N * N) // (2 * te)
    cur = lambda i, s: (0, jnp.minimum(s, nblk - 1), i)
    prev = lambda i, s: (0, jnp.maximum(s - 1, 0), i)
    return pl.pallas_call(
        _peer_kernel,
        grid=(T // tt, nblk + 1),
        in_specs=[
            pl.BlockSpec((D, tt), lambda i, s: (0, i)),
            pl.BlockSpec((2 * te, D), lambda i, s: (jnp.minimum(s, nblk - 1), 0)),
            pl.BlockSpec((D, 2 * te), lambda i, s: (0, jnp.maximum(s - 1, 0))),
            pl.BlockSpec((H, 2 * PEER_NI, tt), cur),
            pl.BlockSpec((H, 2 * PEER_NI, tt), cur),
            pl.BlockSpec((H, 2 * PEER_NI, tt), prev),
            pl.BlockSpec((H, 2 * PEER_NI, tt), prev),
            pl.BlockSpec((H, N, tt), lambda i, s: (0, 0, i)),
            pl.BlockSpec((H, N, tt), lambda i, s: (0, 0, i)),
            pl.BlockSpec((8, LANES), lambda i, s: (0, 0)),
        ],
        out_specs=pl.BlockSpec((D, tt), lambda i, s: (0, i)),
        out_shape=jax.ShapeDtypeStruct((D, T), F32),
        scratch_shapes=[pltpu.VMEM((te, tt), F32)] * 2 + [pltpu.VMEM((te, tt), BF16)] * 2,
        compiler_params=_params("parallel", "arbitrary"),
        name="peer",
    )(xt, u_bf, vt_bf, cnt, e1, cnt, e1, rk, e2, _gelu_consts())


def _vprep_kernel(v_ref, o_ref):
    o_ref[...] = v_ref[...].T.astype(BF16)


def _vprep(v):
    E, D = v.shape
    te = 2048
    return pl.pallas_call(
        _vprep_kernel,
        grid=(E // te,),
        in_specs=[pl.BlockSpec((te, D), lambda i: (i, 0))],
        out_specs=pl.BlockSpec((D, te), lambda i: (0, i)),
        out_shape=jax.ShapeDtypeStruct((D, E), BF16),
        compiler_params=_params("parallel"),
        name="vprep",
    )(v)


def _final_kernel(h_ref, pt_ref, p_ref, wp_ref, wg_ref, gp_ref, gf_ref, o_ref):
    h2 = h_ref[...] + pt_ref[...].T
    ple = _dot(p_ref[...].astype(BF16), wp_ref[...])
    gate = _sigmoid(_dot(_rms(h2, gp_ref[...]).astype(BF16), wg_ref[...]))
    o_ref[...] = _rms(h2 + ple * gate, gf_ref[...])


def _final(h1, peer_t, p2, wp, wg, gp, gf):
    T, D = h1.shape
    tm = min(1024, T)
    full = lambda i: (0, 0)
    return pl.pallas_call(
        _final_kernel,
        grid=(T // tm,),
        in_specs=[
            pl.BlockSpec((tm, D), lambda i: (i, 0)),
            pl.BlockSpec((D, tm), lambda i: (0, i)),
            pl.BlockSpec((tm, PLE_DIM), lambda i: (i, 0)),
            pl.BlockSpec((PLE_DIM, D), full),
            pl.BlockSpec((D, D), full),
            pl.BlockSpec((1, D), full),
            pl.BlockSpec((1, D), full),
        ],
        out_specs=pl.BlockSpec((tm, D), lambda i: (i, 0)),
        out_shape=jax.ShapeDtypeStruct((T, D), F32),
        compiler_params=_params("parallel"),
        name="final",
    )(h1, peer_t, p2, wp, wg, gp, gf)


def kernel(x, p, g_mix, w_in, w_gla_a_up, b_gla_a, g_gla_o, w_gla_out, w_sb_out, w_o, g_ffn, w_peer_q, peer_k1, peer_k2, peer_u, peer_v, g_ple, w_ple_gate, w_ple, g_final):
    B, S, D = x.shape
    T = B * S
    depth = w_in.shape[0]
    assert depth == 1, "the final RMSNorm is fused into the last stage of a single layer"
    qk = GLA_HEADS * GLA_DK
    gv = GLA_HEADS * GLA_DV
    lo = 2 * qk + 2 * gv
    h = x.reshape(T, D)
    for i in range(depth):
        w_main = jnp.concatenate([w_in[i][:, :lo], w_in[i][:, lo + GLA_RANK:]], axis=1).astype(BF16)
        w_ga = jnp.pad(w_in[i][:, lo:lo + GLA_RANK], ((0, 0), (0, LANES - GLA_RANK))).astype(BF16)
        wup = jnp.pad(w_gla_a_up[i], ((0, LANES - GLA_RANK), (0, 0)))

        proj, a_lo = _inproj(h, g_mix[i][None], w_main, w_ga)
        proj3 = proj.reshape(B, S, -1)
        gla_o = _gla(proj3, a_lo.reshape(B, S, LANES), wup, b_gla_a[i][None], g_gla_o[i][None])
        sb_o = _sb(proj3)
        h1, xt = _mix(gla_o.reshape(T, -1), sb_o.reshape(T, -1), proj, h,
                      w_gla_out[i].astype(BF16), w_sb_out[i].astype(BF16), w_o[i].astype(BF16), g_ffn[i][None])
        cnt, e1, rk, e2 = _route(xt, w_peer_q[i].astype(BF16).T, peer_k1[i].astype(BF16), peer_k2[i].astype(BF16))
        peer_t = _peer(xt, peer_u[i].astype(BF16), _vprep(peer_v[i]), cnt, e1, rk, e2)
        h = _final(h1, peer_t, p[i].reshape(T, -1), w_ple[i].astype(BF16), w_ple_gate[i].astype(BF16),
                   g_ple[i][None], g_final[None])
    return h.reshape(B, S, D)
```

```python
import functools
import math

import jax
import jax.numpy as jnp
from jax import lax
from jax.experimental import pallas as pl
from jax.experimental.pallas import tpu as pltpu

F32 = jnp.float32
BF16 = jnp.bfloat16

D_MODEL = 1024
EPS = 1e-6
CHUNK = 64
GLA_HEADS, GLA_DK, GLA_DV, GLA_RANK, GLA_TAU = 4, 128, 256, 16, 16.0
SB_HEADS, SB_DH, SB_BLOCK = 16, 64, 128
PEER_HEADS, PEER_NKEYS, PEER_TOPK = 8, 128, 16
PEER_HALF = 128
PLE_DIM = 256

LANES = 128
VMEM_LIMIT = 56 * 1024 * 1024

SB_EXIT = -64.0

GLA_GROUP = 4 * CHUNK

NEG_INF = float("-inf")


def _params(*sem):
    return pltpu.CompilerParams(dimension_semantics=sem, vmem_limit_bytes=VMEM_LIMIT)


def _rms(x, g):
    return x * lax.rsqrt(jnp.mean(x * x, axis=-1, keepdims=True) + EPS) * g


def _sigmoid(x):
    return 1.0 / (1.0 + jnp.exp(-x))


def _log_sigmoid(x):
    return jnp.minimum(x, 0.0) - jnp.log(1.0 + jnp.exp(-jnp.abs(x)))


def _gelu_tanh(x):
    c = math.sqrt(2.0 / math.pi)
    return 0.5 * x * (1.0 + jnp.tanh(c * (x + 0.044715 * (x * x * x))))


def _gelu_consts():
    k = -2.0 * math.sqrt(2.0 / math.pi) * math.log2(math.e)
    return jnp.zeros((8, LANES), F32).at[0, 0].set(k * 0.044715).at[0, 1].set(k)


def _gelu_bf16(x, k3, k1):
    t = x * (k3 * (x * x) + k1)
    return x * (1.0 / (1.0 + jnp.exp2(t)))


def _dot(a, b):
    return jnp.dot(a, b, preferred_element_type=F32)


def _dot_nt(a, b):
    return lax.dot_general(a, b, (((1,), (1,)), ((), ())), preferred_element_type=F32)


def _dot_tn(a, b):
    return lax.dot_general(a, b, (((0,), (0,)), ((), ())), preferred_element_type=F32)


def _inproj_kernel(x_ref, g_ref, w_ref, wga_ref, o_ref, ga_ref, u_scr):
    @pl.when(pl.program_id(1) == 0)
    def _():
        u_scr[...] = _rms(x_ref[...], g_ref[...]).astype(BF16)
        ga_ref[...] = _dot(u_scr[...], wga_ref[...])

    o_ref[...] = _dot(u_scr[...], w_ref[...]).astype(BF16)


def _inproj(x2, g, w_main, w_ga):
    T = x2.shape[0]
    N = w_main.shape[1]
    tm = min(1024, T)
    tn = 4096
    return pl.pallas_call(
        _inproj_kernel,
        grid=(T // tm, N // tn),
        in_specs=[
            pl.BlockSpec((tm, D_MODEL), lambda i, j: (i, 0)),
            pl.BlockSpec((1, D_MODEL), lambda i, j: (0, 0)),
            pl.BlockSpec((D_MODEL, tn), lambda i, j: (0, j)),
            pl.BlockSpec((D_MODEL, LANES), lambda i, j: (0, 0)),
        ],
        out_specs=[
            pl.BlockSpec((tm, tn), lambda i, j: (i, j)),
            pl.BlockSpec((tm, LANES), lambda i, j: (i, 0)),
        ],
        out_shape=[
            jax.ShapeDtypeStruct((T, N), BF16),
            jax.ShapeDtypeStruct((T, LANES), F32),
        ],
        scratch_shapes=[pltpu.VMEM((tm, D_MODEL), BF16)],
        compiler_params=_params("parallel", "arbitrary"),
        name="inproj",
    )(x2, g, w_main, w_ga)


def _gla_kernel(q_ref, k_ref, v_ref, r_ref, ga_ref, wup_ref, ba_ref, go_ref, o_ref, kd_scr, dec_scr, st_scr):
    S = q_ref.shape[1]
    G = GLA_GROUP
    row = lax.broadcasted_iota(jnp.int32, (G, G), 0)
    col = lax.broadcasted_iota(jnp.int32, (G, G), 1)
    same = (row // CHUNK) == (col // CHUNK)
    cm = jnp.concatenate([jnp.where(same & (col <= row), 1.0, 0.0), jnp.where(same, 1.0, 0.0)], axis=0).astype(BF16)

    ga = ga_ref[0]
    ga_hi = ga.astype(BF16)
    ga_lo = (ga - ga_hi.astype(F32)).astype(BF16)
    wu = wup_ref[...]
    wu_hi = wu.astype(BF16)
    wu_lo = (wu - wu_hi.astype(F32)).astype(BF16)
    pre = (_dot(jnp.concatenate([ga_hi, ga_lo], axis=1), jnp.concatenate([wu_hi, wu_hi], axis=0))
           + _dot(ga_hi, wu_lo) + ba_ref[...])
    log_a = _log_sigmoid(pre) * (1.0 / GLA_TAU)
    hi = log_a.astype(BF16)
    lo = (log_a - hi.astype(F32)).astype(BF16)
    hl = jnp.concatenate([hi, lo], axis=1)
    for g in range(S // G):
        rows = slice(g * G, (g + 1) * G)
        sums = _dot(cm, hl[rows])
        sums = sums[:, :GLA_DK] + sums[:, GLA_DK:]
        logb, tot = sums[:G], sums[G:]
        kd_scr[rows, :] = (k_ref[0, rows, :].astype(F32) * jnp.exp(tot - logb)).astype(BF16)
        dec_scr[rows, :] = jnp.exp(tot)

    WG = 2 * G
    per = WG // CHUNK

    def walk(gi, st):
        base = pl.multiple_of(gi * WG, WG)
        kv = [_dot_tn(v_ref[0, pl.ds(base + c * CHUNK, CHUNK), :], kd_scr[pl.ds(base + c * CHUNK, CHUNK), :])
              for c in range(per)]
        for c in range(per):
            st = dec_scr[pl.ds(base + c * CHUNK, 1), :] * st + kv[c]
            st_scr[gi * per + c] = st.astype(BF16)
        return st

    lax.fori_loop(0, S // WG, walk, jnp.zeros((GLA_DV, GLA_DK), F32))

    def emit(gi, _):
        base = pl.multiple_of(gi * WG, WG)
        o = jnp.concatenate([_dot_nt(q_ref[0, pl.ds(base + c * CHUNK, CHUNK), :], st_scr[gi * per + c])
                             for c in range(per)], axis=0) * (GLA_DK ** -0.5)
        o = _rms(o, go_ref[...])
        rows = pl.ds(base, WG)
        r = r_ref[0, rows, :].astype(F32)
        o_ref[0, rows, :] = (o * (r * _sigmoid(r))).astype(BF16)
        return 0

    lax.fori_loop(0, S // WG, emit, 0)


def _gla(proj3, ga3, wup, ba, go):
    B, S, _ = proj3.shape
    H = GLA_HEADS
    kb = (H * GLA_DK) // GLA_DK
    vb = (2 * H * GLA_DK) // GLA_DV
    rb = vb + H
    return pl.pallas_call(
        _gla_kernel,
        grid=(B, H),
        in_specs=[
            pl.BlockSpec((1, S, GLA_DK), lambda b, h: (b, 0, h)),
            pl.BlockSpec((1, S, GLA_DK), lambda b, h: (b, 0, kb + h)),
            pl.BlockSpec((1, S, GLA_DV), lambda b, h: (b, 0, vb + h)),
            pl.BlockSpec((1, S, GLA_DV), lambda b, h: (b, 0, rb + h)),
            pl.BlockSpec((1, S, LANES), lambda b, h: (b, 0, 0)),
            pl.BlockSpec((LANES, GLA_DK), lambda b, h: (0, h)),
            pl.BlockSpec((1, GLA_DK), lambda b, h: (0, h)),
            pl.BlockSpec((1, GLA_DV), lambda b, h: (0, h)),
        ],
        out_specs=pl.BlockSpec((1, S, GLA_DV), lambda b, h: (b, 0, h)),
        out_shape=jax.ShapeDtypeStruct((B, S, H * GLA_DV), BF16),
        scratch_shapes=[
            pltpu.VMEM((S, GLA_DK), BF16),
            pltpu.VMEM((S, GLA_DK), F32),
            pltpu.VMEM((S // CHUNK, GLA_DV, GLA_DK), BF16),
        ],
        compiler_params=_params("parallel", "parallel"),
        name="gla",
    )(proj3, proj3, proj3, proj3, ga3, wup, ba, go)


SB_PAIRS = 8


def _sb_kernel(q_ref, k_ref, v_ref, o_ref, acc_ref, car_ref):
    i = pl.program_id(2)
    Q = SB_BLOCK
    nch = 2 * SB_PAIRS
    row = lax.broadcasted_iota(jnp.int32, (Q, Q), 0)
    col = lax.broadcasted_iota(jnp.int32, (Q, Q), 1)
    lane = lax.broadcasted_iota(jnp.int32, (Q, LANES), 1)
    uo = jnp.concatenate([jnp.where(row > col, 1.0, 0.0), jnp.ones((Q, Q), F32)], axis=1).astype(BF16)
    uo2 = jnp.concatenate([uo, uo], axis=0)
    lower = lane < SB_DH
    srow = lax.broadcasted_iota(jnp.int32, (nch * Q, Q), 0)
    scol = lax.broadcasted_iota(jnp.int32, (nch * Q, Q), 1)

    qs = []
    for p in range(SB_PAIRS):
        q2 = q_ref[0, :, p * LANES:(p + 1) * LANES] * (SB_DH ** -0.5)
        zero = jnp.zeros_like(q2)
        qs.append(jnp.where(lower, q2, zero))
        qs.append(jnp.where(lower, zero, q2))

    def block(j, diag):
        rows = pl.ds(pl.multiple_of(j * Q, Q), Q)
        kj = k_ref[0, rows, :]
        vj = v_ref[0, rows, :]
        z = jnp.concatenate([_dot_nt(qs[c], kj[:, (c // 2) * LANES:(c // 2 + 1) * LANES])
                             for c in range(nch)], axis=0)
        ls = _log_sigmoid(z)
        stay = ls - z
        if diag:
            mask = scol < (srow & (Q - 1))
            stay = jnp.where(mask, stay, 0.0)
        hi = stay.astype(BF16)
        lo = (stay - hi.astype(F32)).astype(BF16)
        sums = _dot(jnp.concatenate([hi, lo], axis=1), uo2)
        if diag:
            w = jnp.where(mask, jnp.exp(ls + sums[:, :Q]), 0.0)
            car = sums[:, Q:]
        else:
            w = jnp.exp(ls + sums[:, :Q] + car_ref[...])
            car = car_ref[...] + sums[:, Q:]
        car_ref[...] = car
        wb = w.astype(BF16)
        pv = jnp.concatenate([_dot(wb[c * Q:(c + 1) * Q], vj[:, (c // 2) * LANES:(c // 2 + 1) * LANES])
                              for c in range(nch)], axis=0)
        if diag:
            acc_ref[...] = pv
        else:
            acc_ref[...] += pv
        return jnp.max(car)

    worst0 = block(i, True)

    def cond(s):
        j, worst = s
        return (j >= 0) & (worst > SB_EXIT)

    def body(s):
        j, _ = s
        return j - 1, block(j, False)

    lax.while_loop(cond, body, (i - 1, worst0))

    for p in range(SB_PAIRS):
        even = acc_ref[2 * p * Q:(2 * p + 1) * Q, :]
        odd = acc_ref[(2 * p + 1) * Q:(2 * p + 2) * Q, :]
        o_ref[0, :, p * LANES:(p + 1) * LANES] = jnp.where(lower, even, odd).astype(BF16)


def _sb(proj3):
    B, S, _ = proj3.shape
    W = SB_PAIRS * LANES
    base = (2 * GLA_HEADS * GLA_DK + 2 * GLA_HEADS * GLA_DV) // W
    nb = (SB_HEADS * SB_DH) // W
    return pl.pallas_call(
        _sb_kernel,
        grid=(B, nb, S // SB_BLOCK),
        in_specs=[
            pl.BlockSpec((1, SB_BLOCK, W), lambda b, p, i: (b, i, base + p)),
            pl.BlockSpec((1, S, W), lambda b, p, i: (b, 0, base + nb + p)),
            pl.BlockSpec((1, S, W), lambda b, p, i: (b, 0, base + 2 * nb + p)),
        ],
        out_specs=pl.BlockSpec((1, SB_BLOCK, W), lambda b, p, i: (b, i, p)),
        out_shape=jax.ShapeDtypeStruct((B, S, SB_HEADS * SB_DH), BF16),
        scratch_shapes=[
            pltpu.VMEM((2 * SB_PAIRS * SB_BLOCK, LANES), F32),
            pltpu.VMEM((2 * SB_PAIRS * SB_BLOCK, SB_BLOCK), F32),
        ],
        compiler_params=_params("parallel", "parallel", "arbitrary"),
        name="sb",
    )(proj3, proj3, proj3)


def _mix_kernel(gla_ref, sb_ref, ga_ref, gb_ref, x_ref, wa_ref, wb_ref, wo_ref, g_ref, h_ref, xt_ref):
    ya = _dot(gla_ref[...], wa_ref[...])
    yb = _dot(sb_ref[...], wb_ref[...])
    mixed = _sigmoid(ga_ref[...].astype(F32)) * ya + _sigmoid(gb_ref[...].astype(F32)) * yb
    h = x_ref[...] + _dot(mixed.astype(BF16), wo_ref[...])
    h_ref[...] = h
    xt_ref[...] = _rms(h, g_ref[...]).T.astype(BF16)


def _mix(gla_o, sb_o, proj, x2, wa, wb, wo, g):
    T = x2.shape[0]
    tm = min(512, T)
    D = D_MODEL
    gate_blk = (proj.shape[1] - 2 * D) // D
    full = lambda i: (0, 0)
    return pl.pallas_call(
        _mix_kernel,
        grid=(T // tm,),
        in_specs=[
            pl.BlockSpec((tm, D), lambda i: (i, 0)),
            pl.BlockSpec((tm, D), lambda i: (i, 0)),
            pl.BlockSpec((tm, D), lambda i: (i, gate_blk)),
            pl.BlockSpec((tm, D), lambda i: (i, gate_blk + 1)),
            pl.BlockSpec((tm, D), lambda i: (i, 0)),
            pl.BlockSpec((D, D), full),
            pl.BlockSpec((D, D), full),
            pl.BlockSpec((D, D), full),
            pl.BlockSpec((1, D), full),
        ],
        out_specs=[
            pl.BlockSpec((tm, D), lambda i: (i, 0)),
            pl.BlockSpec((D, tm), lambda i: (0, i)),
        ],
        out_shape=[
            jax.ShapeDtypeStruct((T, D), F32),
            jax.ShapeDtypeStruct((D, T), BF16),
        ],
        compiler_params=_params("parallel"),
        name="mix",
    )(gla_o, sb_o, proj, proj, x2, wa, wb, wo, g)


def _sort16_network():
    pairs = []

    def merge(lo, n, r):
        step = 2 * r
        if step < n:
            merge(lo, n, step)
            merge(lo + r, n, step)
            pairs.extend((i, i + r) for i in range(lo + r, lo + n - r, step))
        else:
            pairs.append((lo, lo + r))

    def sort(lo, n):
        if n > 1:
            sort(lo, n // 2)
            sort(lo + n // 2, n // 2)
            merge(lo, n, 1)

    sort(0, 16)
    return tuple(pairs)


_SORT16 = _sort16_network()
_BITONIC16 = tuple((i, i + d) for d in (8, 4, 2, 1) for i in range(16) if not i & d)


def _exchange(x, pairs):
    for i, j in pairs:
        x[i], x[j] = jnp.maximum(x[i], x[j]), jnp.minimum(x[i], x[j])


def _merge_sublanes(x):
    for shift in (4, 2, 1):
        y = [pltpu.roll(v, shift, 0) for v in x]
        x = [jnp.maximum(x[i], y[15 - i]) for i in range(16)]
        _exchange(x, _BITONIC16)
    return x


def _route_kernel(xt_ref, wq_ref, k1_ref, k2_ref, cnt_ref, e1_ref, rk_ref, e2_ref, q_scr):
    K = PEER_TOPK
    tt = xt_ref.shape[1]
    sub = lax.broadcasted_iota(jnp.int32, (8, tt), 0)
    q_scr[...] = _dot(wq_ref[...], xt_ref[...]).astype(BF16)

    def head(h, _):
        qt = q_scr[pl.ds(pl.multiple_of(h * 2 * PEER_HALF, 2 * PEER_HALF), 2 * PEER_HALF), :]
        s1 = _dot(k1_ref[h], qt[:PEER_HALF])
        s2 = _dot(k2_ref[h], qt[PEER_HALF:])
        s1 = [s1[8 * k:8 * k + 8] for k in range(16)]
        s2 = [s2[8 * k:8 * k + 8] for k in range(16)]
        v1, v2 = list(s1), list(s2)
        _exchange(v1, _SORT16)
        _exchange(v2, _SORT16)
        v1 = _merge_sublanes(v1)
        v2 = _merge_sublanes(v2)

        lo8, hi8 = v1[0], v1[8]
        for a in range(1, 8):
            lo8 = jnp.where(sub == a, v1[a], lo8)
            hi8 = jnp.where(sub == a, v1[8 + a], hi8)
        p = [lo8 + v2[0]] + [jnp.where(sub < K // (b + 1), lo8 + v2[b], NEG_INF) for b in range(1, K)]
        ph = hi8 + v2[0]
        top = list(p)
        top[K - 1] = jnp.maximum(top[K - 1], ph)
        _exchange(top, _BITONIC16)
        top = _merge_sublanes(top)
        thr = top[K - 1]
        z = jnp.ones_like(thr)
        for i in range(1, K):
            z = z + jnp.exp(top[i] - top[0])
        inv_z = 1.0 / z

        pairs_lo = jnp.where(p[0] >= thr, 1.0, 0.0)
        for b in range(1, K):
            pairs_lo = pairs_lo + jnp.where(p[b] >= thr, 1.0, 0.0)
        pairs_hi = jnp.where(ph >= thr, 1.0, 0.0)
        pairs = [jnp.broadcast_to((pairs_lo if a < 8 else pairs_hi)[a % 8:a % 8 + 1], (8, tt)) for a in range(K)]

        for k in range(0, 16, 2):
            cnt, rk = [], []
            for kk in (k, k + 1):
                c = jnp.zeros((8, tt), F32)
                r = jnp.zeros((8, tt), F32)
                for a in range(K):
                    c = jnp.where(s1[kk] == v1[a], pairs[a], c)
                    r = jnp.where(v2[a] > s2[kk], a + 1.0, r)
                cnt.append(c)
                rk.append(r)
            rows = slice(8 * k, 8 * k + 16)
            cnt_ref[h, rows, :] = jnp.concatenate(cnt, axis=0)
            rk_ref[h, rows, :] = jnp.concatenate(rk, axis=0).astype(BF16)
            e1_ref[h, rows, :] = jnp.concatenate([jnp.exp(s1[k] - v1[0]) * inv_z,
                                                  jnp.exp(s1[k + 1] - v1[0]) * inv_z], axis=0)
            e2_ref[h, rows, :] = jnp.concatenate([jnp.exp(s2[k] - v2[0]),
                                                  jnp.exp(s2[k + 1] - v2[0])], axis=0).astype(BF16)
        return 0

    lax.fori_loop(0, PEER_HEADS, head, 0)


def _route(xt, wq_t, k1, k2):
    D, T = xt.shape
    H, N = PEER_HEADS, PEER_NKEYS
    tt = min(1024, T)
    return pl.pallas_call(
        _route_kernel,
        grid=(T // tt,),
        in_specs=[
            pl.BlockSpec((D, tt), lambda i: (0, i)),
            pl.BlockSpec(wq_t.shape, lambda i: (0, 0)),
            pl.BlockSpec(k1.shape, lambda i: (0, 0, 0)),
            pl.BlockSpec(k2.shape, lambda i: (0, 0, 0)),
        ],
        out_specs=[pl.BlockSpec((H, N, tt), lambda i: (0, 0, i))] * 4,
        out_shape=[
            jax.ShapeDtypeStruct((H, N, T), F32),
            jax.ShapeDtypeStruct((H, N, T), F32),
            jax.ShapeDtypeStruct((H, N, T), BF16),
            jax.ShapeDtypeStruct((H, N, T), BF16),
        ],
        scratch_shapes=[pltpu.VMEM((wq_t.shape[0], tt), BF16)],
        compiler_params=_params("parallel"),
        name="route",
    )(xt, wq_t, k1, k2)


PEER_NI = 4
PEER_TT = 1024


def _peer_kernel(xt_ref, u_ref, vt_ref, cnt_cur, e1_cur, cnt_prev, e1_prev, rk_ref, e2_ref, gk_ref, o_ref,
                 act0, act1, a_cat):
    s = pl.program_id(1)
    N = PEER_NKEYS
    te = PEER_NI * N
    a0 = a_cat.at[0:te]
    a1 = a_cat.at[te:2 * te]
    k3 = gk_ref[0:1, 0:1].astype(BF16)
    k1 = gk_ref[0:1, 1:2].astype(BF16)

    def scores(half, act_ref):
        act_ref[...] = _dot(u_ref[half * te:(half + 1) * te, :], xt_ref[...])

    def gates(cnt_ref, e1_ref, half, n, act_ref, a_ref):
        W = 2 * LANES
        for lg in range(xt_ref.shape[1] // W):
            ls = slice(lg * W, (lg + 1) * W)
            row = slice(half * PEER_NI + n, half * PEER_NI + n + 1)
            g = jnp.zeros((N, W), BF16)
            for h in range(PEER_HEADS):
                cb = jnp.broadcast_to(cnt_ref[h, row, ls], g.shape).astype(BF16)
                eb = jnp.broadcast_to(e1_ref[h, row, ls], g.shape).astype(BF16)
                g = g + jnp.where(rk_ref[h, :, ls] < cb, e2_ref[h, :, ls], jnp.zeros((), BF16)) * eb
            a_ref[n * N:(n + 1) * N, ls] = _gelu_bf16(act_ref[n * N:(n + 1) * N, ls].astype(BF16), k3, k1) * g

    def mix(half, a_ref):
        o_ref[...] += _dot(vt_ref[:, half * te:(half + 1) * te], a_ref[...])

    last = pl.num_programs(1) - 1

    @pl.when(s == 0)
    def _():
        o_ref[...] = jnp.zeros_like(o_ref)
        scores(0, act0)
        scores(1, act1)
        for n in range(PEER_NI):
            gates(cnt_cur, e1_cur, 0, n, act0, a0)

    def mix_all():
        o_ref[...] += _dot(vt_ref[...], a_cat[...])

    @pl.when((s > 0) & (s < last))
    def _():
        scores(0, act0)
        for n in range(PEER_NI):
            gates(cnt_prev, e1_prev, 1, n, act1, a1)
        mix_all()
        scores(1, act1)
        for n in range(PEER_NI):
            gates(cnt_cur, e1_cur, 0, n, act0, a0)

    @pl.when(s == last)
    def _():
        for n in range(PEER_NI):
            gates(cnt_prev, e1_prev, 1, n, act1, a1)
        mix_all()


def _peer(xt, u_bf, vt_bf, cnt, e1, rk, e2):
    D, T = xt.shape
    H, N = PEER_HEADS, PEER_NKEYS
    tt = min(PEER_TT, T)
    te = PEER_NI * N
    nblk = (N * N) // (2 * te)
    cur = lambda i, s: (0, jnp.minimum(s, nblk - 1), i)
    prev = lambda i, s: (0, jnp.maximum(s - 1, 0), i)
    return pl.pallas_call(
        _peer_kernel,
        grid=(T // tt, nblk + 1),
        in_specs=[
            pl.BlockSpec((D, tt), lambda i, s: (0, i)),
            pl.BlockSpec((2 * te, D), lambda i, s: (jnp.minimum(s, nblk - 1), 0)),
            pl.BlockSpec((D, 2 * te), lambda i, s: (0, jnp.maximum(s - 1, 0))),
            pl.BlockSpec((H, 2 * PEER_NI, tt), cur),
            pl.BlockSpec((H, 2 * PEER_NI, tt), cur),
            pl.BlockSpec((H, 2 * PEER_NI, tt), prev),
            pl.BlockSpec((H, 2 * PEER_NI, tt), prev),
            pl.BlockSpec((H, N, tt), lambda i, s: (0, 0, i)),
            pl.BlockSpec((H, N, tt), lambda i, s: (0, 0, i)),
            pl.BlockSpec((8, LANES), lambda i, s: (0, 0)),
        ],
        out_specs=pl.BlockSpec((D, tt), lambda i, s: (0, i)),
        out_shape=jax.ShapeDtypeStruct((D, T), F32),
        scratch_shapes=[pltpu.VMEM((te, tt), F32)] * 2 + [pltpu.VMEM((2 * te, tt), BF16)],
        compiler_params=_params("parallel", "arbitrary"),
        name="peer",
    )(xt, u_bf, vt_bf, cnt, e1, cnt, e1, rk, e2, _gelu_consts())


def _vprep_kernel(v_ref, o_ref):
    o_ref[...] = v_ref[...].T.astype(BF16)


def _vprep(v):
    E, D = v.shape
    te = 2048
    return pl.pallas_call(
        _vprep_kernel,
        grid=(E // te,),
        in_specs=[pl.BlockSpec((te, D), lambda i: (i, 0))],
        out_specs=pl.BlockSpec((D, te), lambda i: (0, i)),
        out_shape=jax.ShapeDtypeStruct((D, E), BF16),
        compiler_params=_params("parallel"),
        name="vprep",
    )(v)


def _final_kernel(h_ref, pt_ref, p_ref, wp_ref, wg_ref, gp_ref, gf_ref, o_ref):
    h2 = h_ref[...] + pt_ref[...].T
    ple = _dot(p_ref[...].astype(BF16), wp_ref[...])
    gate = _sigmoid(_dot(_rms(h2, gp_ref[...]).astype(BF16), wg_ref[...]))
    o_ref[...] = _rms(h2 + ple * gate, gf_ref[...])


def _final(h1, peer_t, p2, wp, wg, gp, gf):
    T, D = h1.shape
    tm = min(1024, T)
    full = lambda i: (0, 0)
    return pl.pallas_call(
        _final_kernel,
        grid=(T // tm,),
        in_specs=[
            pl.BlockSpec((tm, D), lambda i: (i, 0)),
            pl.BlockSpec((D, tm), lambda i: (0, i)),
            pl.BlockSpec((tm, PLE_DIM), lambda i: (i, 0)),
            pl.BlockSpec((PLE_DIM, D), full),
            pl.BlockSpec((D, D), full),
            pl.BlockSpec((1, D), full),
            pl.BlockSpec((1, D), full),
        ],
        out_specs=pl.BlockSpec((tm, D), lambda i: (i, 0)),
        out_shape=jax.ShapeDtypeStruct((T, D), F32),
        compiler_params=_params("parallel"),
        name="final",
    )(h1, peer_t, p2, wp, wg, gp, gf)


def kernel(x, p, g_mix, w_in, w_gla_a_up, b_gla_a, g_gla_o, w_gla_out, w_sb_out, w_o, g_ffn, w_peer_q, peer_k1, peer_k2, peer_u, peer_v, g_ple, w_ple_gate, w_ple, g_final):
    B, S, D = x.shape
    T = B * S
    depth = w_in.shape[0]
    assert depth == 1, "the final RMSNorm is fused into the last stage of a single layer"
    qk = GLA_HEADS * GLA_DK
    gv = GLA_HEADS * GLA_DV
    lo = 2 * qk + 2 * gv
    h = x.reshape(T, D)
    for i in range(depth):
        w_main = jnp.concatenate([w_in[i][:, :lo], w_in[i][:, lo + GLA_RANK:]], axis=1).astype(BF16)
        w_ga = jnp.pad(w_in[i][:, lo:lo + GLA_RANK], ((0, 0), (0, LANES - GLA_RANK))).astype(BF16)
        wup = jnp.pad(w_gla_a_up[i], ((0, LANES - GLA_RANK), (0, 0)))

        proj, a_lo = _inproj(h, g_mix[i][None], w_main, w_ga)
        proj3 = proj.reshape(B, S, -1)
        gla_o = _gla(proj3, a_lo.reshape(B, S, LANES), wup, b_gla_a[i][None], g_gla_o[i][None])
        sb_o = _sb(proj3)
        h1, xt = _mix(gla_o.reshape(T, -1), sb_o.reshape(T, -1), proj, h,
                      w_gla_out[i].astype(BF16), w_sb_out[i].astype(BF16), w_o[i].astype(BF16), g_ffn[i][None])
        cnt, e1, rk, e2 = _route(xt, w_peer_q[i].astype(BF16).T, peer_k1[i].astype(BF16), peer_k2[i].astype(BF16))
        peer_t = _peer(xt, peer_u[i].astype(BF16), _vprep(peer_v[i]), cnt, e1, rk, e2)
        h = _final(h1, peer_t, p[i].reshape(T, -1), w_ple[i].astype(BF16), w_ple_gate[i].astype(BF16),
                   g_ple[i][None], g_final[None])
    return h.reshape(B, S, D)
```

```python
import math

import jax
import jax.numpy as jnp
from jax import lax
from jax.experimental import pallas as pl
from jax.experimental.pallas import tpu as pltpu

F32 = jnp.float32
BF16 = jnp.bfloat16

D_MODEL = 1024
EPS = 1e-6
CHUNK = 64
GLA_HEADS, GLA_DK, GLA_DV, GLA_RANK, GLA_TAU = 4, 128, 256, 16, 16.0
SB_HEADS, SB_DH, SB_BLOCK = 16, 64, 128
PEER_HEADS, PEER_NKEYS, PEER_TOPK = 8, 128, 16
PEER_HALF = 128
PLE_DIM = 256

LANES = 128
VMEM_LIMIT = 56 * 1024 * 1024

SB_EXIT = -64.0

GLA_GROUP = 4 * CHUNK

NEG_INF = float("-inf")


def _params(*sem):
    return pltpu.CompilerParams(dimension_semantics=sem, vmem_limit_bytes=VMEM_LIMIT)


def _rms(x, g):
    return x * lax.rsqrt(jnp.mean(x * x, axis=-1, keepdims=True) + EPS) * g


def _sigmoid(x):
    return 1.0 / (1.0 + jnp.exp(-x))


def _log_sigmoid(x):
    return jnp.minimum(x, 0.0) - jnp.log(1.0 + jnp.exp(-jnp.abs(x)))


def _gelu_tanh(x):
    c = math.sqrt(2.0 / math.pi)
    return 0.5 * x * (1.0 + jnp.tanh(c * (x + 0.044715 * (x * x * x))))


def _gelu_consts():
    k = -2.0 * math.sqrt(2.0 / math.pi) * math.log2(math.e)
    return jnp.zeros((8, LANES), F32).at[0, 0].set(k * 0.044715).at[0, 1].set(k)


def _gelu_bf16(x, k3, k1):
    t = x * (k3 * (x * x) + k1)
    return x * (1.0 / (1.0 + jnp.exp2(t)))


def _dot(a, b):
    return jnp.dot(a, b, preferred_element_type=F32)


def _dot_nt(a, b):
    return lax.dot_general(a, b, (((1,), (1,)), ((), ())), preferred_element_type=F32)


def _dot_tn(a, b):
    return lax.dot_general(a, b, (((0,), (0,)), ((), ())), preferred_element_type=F32)


def _inproj_kernel(x_ref, g_ref, w_ref, wga_ref, o_ref, ga_ref, u_scr):
    @pl.when(pl.program_id(1) == 0)
    def _():
        u_scr[...] = _rms(x_ref[...], g_ref[...]).astype(BF16)
        ga_ref[...] = _dot(u_scr[...], wga_ref[...])

    o_ref[...] = _dot(u_scr[...], w_ref[...]).astype(BF16)


def _inproj(x2, g, w_main, w_ga):
    T = x2.shape[0]
    N = w_main.shape[1]
    tm = min(1024, T)
    tn = 4096
    return pl.pallas_call(
        _inproj_kernel,
        grid=(T // tm, N // tn),
        in_specs=[
            pl.BlockSpec((tm, D_MODEL), lambda i, j: (i, 0)),
            pl.BlockSpec((1, D_MODEL), lambda i, j: (0, 0)),
            pl.BlockSpec((D_MODEL, tn), lambda i, j: (0, j)),
            pl.BlockSpec((D_MODEL, LANES), lambda i, j: (0, 0)),
        ],
        out_specs=[
            pl.BlockSpec((tm, tn), lambda i, j: (i, j)),
            pl.BlockSpec((tm, LANES), lambda i, j: (i, 0)),
        ],
        out_shape=[
            jax.ShapeDtypeStruct((T, N), BF16),
            jax.ShapeDtypeStruct((T, LANES), F32),
        ],
        scratch_shapes=[pltpu.VMEM((tm, D_MODEL), BF16)],
        compiler_params=_params("parallel", "arbitrary"),
        name="inproj",
    )(x2, g, w_main, w_ga)


def _gla_kernel(q_ref, k_ref, v_ref, r_ref, ga_ref, wup_ref, ba_ref, go_ref, o_ref, kd_scr, dec_scr, st_scr):
    S = q_ref.shape[1]
    G = GLA_GROUP
    row = lax.broadcasted_iota(jnp.int32, (G, G), 0)
    col = lax.broadcasted_iota(jnp.int32, (G, G), 1)
    same = (row // CHUNK) == (col // CHUNK)
    cm = jnp.concatenate([jnp.where(same & (col <= row), 1.0, 0.0), jnp.where(same, 1.0, 0.0)], axis=0).astype(BF16)

    ga = ga_ref[0]
    ga_hi = ga.astype(BF16)
    ga_lo = (ga - ga_hi.astype(F32)).astype(BF16)
    wu = wup_ref[...]
    wu_hi = wu.astype(BF16)
    wu_lo = (wu - wu_hi.astype(F32)).astype(BF16)
    pre = (_dot(jnp.concatenate([ga_hi, ga_lo], axis=1), jnp.concatenate([wu_hi, wu_hi], axis=0))
           + _dot(ga_hi, wu_lo) + ba_ref[...])
    log_a = _log_sigmoid(pre) * (1.0 / GLA_TAU)
    hi = log_a.astype(BF16)
    lo = (log_a - hi.astype(F32)).astype(BF16)
    hl = jnp.concatenate([hi, lo], axis=1)
    for g in range(S // G):
        rows = slice(g * G, (g + 1) * G)
        sums = _dot(cm, hl[rows])
        sums = sums[:, :GLA_DK] + sums[:, GLA_DK:]
        logb, tot = sums[:G], sums[G:]
        kd_scr[rows, :] = (k_ref[0, rows, :].astype(F32) * jnp.exp(tot - logb)).astype(BF16)
        dec_scr[rows, :] = jnp.exp(tot)

    WG = 2 * G
    per = WG // CHUNK

    def walk(gi, st):
        base = pl.multiple_of(gi * WG, WG)
        kv = [_dot_tn(v_ref[0, pl.ds(base + c * CHUNK, CHUNK), :], kd_scr[pl.ds(base + c * CHUNK, CHUNK), :])
              for c in range(per)]
        for c in range(per):
            st = dec_scr[pl.ds(base + c * CHUNK, 1), :] * st + kv[c]
            st_scr[gi * per + c] = st.astype(BF16)
        return st

    lax.fori_loop(0, S // WG, walk, jnp.zeros((GLA_DV, GLA_DK), F32))

    def emit(gi, _):
        base = pl.multiple_of(gi * WG, WG)
        o = jnp.concatenate([_dot_nt(q_ref[0, pl.ds(base + c * CHUNK, CHUNK), :], st_scr[gi * per + c])
                             for c in range(per)], axis=0) * (GLA_DK ** -0.5)
        o = _rms(o, go_ref[...])
        rows = pl.ds(base, WG)
        r = r_ref[0, rows, :].astype(F32)
        o_ref[0, rows, :] = (o * (r * _sigmoid(r))).astype(BF16)
        return 0

    lax.fori_loop(0, S // WG, emit, 0)


def _gla(proj3, ga3, wup, ba, go):
    B, S, _ = proj3.shape
    H = GLA_HEADS
    kb = (H * GLA_DK) // GLA_DK
    vb = (2 * H * GLA_DK) // GLA_DV
    rb = vb + H
    return pl.pallas_call(
        _gla_kernel,
        grid=(B, H),
        in_specs=[
            pl.BlockSpec((1, S, GLA_DK), lambda b, h: (b, 0, h)),
            pl.BlockSpec((1, S, GLA_DK), lambda b, h: (b, 0, kb + h)),
            pl.BlockSpec((1, S, GLA_DV), lambda b, h: (b, 0, vb + h)),
            pl.BlockSpec((1, S, GLA_DV), lambda b, h: (b, 0, rb + h)),
            pl.BlockSpec((1, S, LANES), lambda b, h: (b, 0, 0)),
            pl.BlockSpec((LANES, GLA_DK), lambda b, h: (0, h)),
            pl.BlockSpec((1, GLA_DK), lambda b, h: (0, h)),
            pl.BlockSpec((1, GLA_DV), lambda b, h: (0, h)),
        ],
        out_specs=pl.BlockSpec((1, S, GLA_DV), lambda b, h: (b, 0, h)),
        out_shape=jax.ShapeDtypeStruct((B, S, H * GLA_DV), BF16),
        scratch_shapes=[
            pltpu.VMEM((S, GLA_DK), BF16),
            pltpu.VMEM((S, GLA_DK), F32),
            pltpu.VMEM((S // CHUNK, GLA_DV, GLA_DK), BF16),
        ],
        compiler_params=_params("parallel", "parallel"),
        name="gla",
    )(proj3, proj3, proj3, proj3, ga3, wup, ba, go)


SB_PAIRS = 8


def _sb_kernel(q_ref, k_ref, v_ref, o_ref, acc_ref, car_ref):
    i = pl.program_id(2)
    Q = SB_BLOCK
    nch = 2 * SB_PAIRS
    row = lax.broadcasted_iota(jnp.int32, (Q, Q), 0)
    col = lax.broadcasted_iota(jnp.int32, (Q, Q), 1)
    lane = lax.broadcasted_iota(jnp.int32, (Q, LANES), 1)
    uo = jnp.concatenate([jnp.where(row > col, 1.0, 0.0), jnp.ones((Q, Q), F32)], axis=1).astype(BF16)
    uo2 = jnp.concatenate([uo, uo], axis=0)
    lower = lane < SB_DH
    srow = lax.broadcasted_iota(jnp.int32, (nch * Q, Q), 0)
    scol = lax.broadcasted_iota(jnp.int32, (nch * Q, Q), 1)

    qs = []
    for p in range(SB_PAIRS):
        q2 = q_ref[0, :, p * LANES:(p + 1) * LANES] * (SB_DH ** -0.5)
        zero = jnp.zeros_like(q2)
        qs.append(jnp.where(lower, q2, zero))
        qs.append(jnp.where(lower, zero, q2))

    def block(j, diag):
        rows = pl.ds(pl.multiple_of(j * Q, Q), Q)
        kj = k_ref[0, rows, :]
        vj = v_ref[0, rows, :]
        z = jnp.concatenate([_dot_nt(qs[c], kj[:, (c // 2) * LANES:(c // 2 + 1) * LANES])
                             for c in range(nch)], axis=0)
        ls = _log_sigmoid(z)
        stay = ls - z
        if diag:
            mask = scol < (srow & (Q - 1))
            stay = jnp.where(mask, stay, 0.0)
        hi = stay.astype(BF16)
        lo = (stay - hi.astype(F32)).astype(BF16)
        sums = _dot(jnp.concatenate([hi, lo], axis=1), uo2)
        if diag:
            w = jnp.where(mask, jnp.exp(ls + sums[:, :Q]), 0.0)
            car = sums[:, Q:]
        else:
            w = jnp.exp(ls + sums[:, :Q] + car_ref[...])
            car = car_ref[...] + sums[:, Q:]
        car_ref[...] = car
        wb = w.astype(BF16)
        pv = jnp.concatenate([_dot(wb[c * Q:(c + 1) * Q], vj[:, (c // 2) * LANES:(c // 2 + 1) * LANES])
                              for c in range(nch)], axis=0)
        if diag:
            acc_ref[...] = pv
        else:
            acc_ref[...] += pv
        return jnp.max(car)

    worst0 = block(i, True)

    def cond(s):
        j, worst = s
        return (j >= 0) & (worst > SB_EXIT)

    def body(s):
        j, _ = s
        return j - 1, block(j, False)

    lax.while_loop(cond, body, (i - 1, worst0))

    for p in range(SB_PAIRS):
        even = acc_ref[2 * p * Q:(2 * p + 1) * Q, :]
        odd = acc_ref[(2 * p + 1) * Q:(2 * p + 2) * Q, :]
        o_ref[0, :, p * LANES:(p + 1) * LANES] = jnp.where(lower, even, odd).astype(BF16)


def _sb(proj3):
    B, S, _ = proj3.shape
    W = SB_PAIRS * LANES
    base = (2 * GLA_HEADS * GLA_DK + 2 * GLA_HEADS * GLA_DV) // W
    nb = (SB_HEADS * SB_DH) // W
    return pl.pallas_call(
        _sb_kernel,
        grid=(B, nb, S // SB_BLOCK),
        in_specs=[
            pl.BlockSpec((1, SB_BLOCK, W), lambda b, p, i: (b, i, base + p)),
            pl.BlockSpec((1, S, W), lambda b, p, i: (b, 0, base + nb + p)),
            pl.BlockSpec((1, S, W), lambda b, p, i: (b, 0, base + 2 * nb + p)),
        ],
        out_specs=pl.BlockSpec((1, SB_BLOCK, W), lambda b, p, i: (b, i, p)),
        out_shape=jax.ShapeDtypeStruct((B, S, SB_HEADS * SB_DH), BF16),
        scratch_shapes=[
            pltpu.VMEM((2 * SB_PAIRS * SB_BLOCK, LANES), F32),
            pltpu.VMEM((2 * SB_PAIRS * SB_BLOCK, SB_BLOCK), F32),
        ],
        compiler_params=_params("parallel", "parallel", "arbitrary"),
        name="sb",
    )(proj3, proj3, proj3)


def _mix_kernel(gla_ref, sb_ref, ga_ref, gb_ref, x_ref, wa_ref, wb_ref, wo_ref, g_ref, h_ref, xt_ref):
    ya = _dot(gla_ref[...], wa_ref[...])
    yb = _dot(sb_ref[...], wb_ref[...])
    mixed = _sigmoid(ga_ref[...].astype(F32)) * ya + _sigmoid(gb_ref[...].astype(F32)) * yb
    h = x_ref[...] + _dot(mixed.astype(BF16), wo_ref[...])
    h_ref[...] = h
    xt_ref[...] = _rms(h, g_ref[...]).T.astype(BF16)


def _mix(gla_o, sb_o, proj, x2, wa, wb, wo, g):
    T = x2.shape[0]
    tm = min(512, T)
    D = D_MODEL
    gate_blk = (proj.shape[1] - 2 * D) // D
    full = lambda i: (0, 0)
    return pl.pallas_call(
        _mix_kernel,
        grid=(T // tm,),
        in_specs=[
            pl.BlockSpec((tm, D), lambda i: (i, 0)),
            pl.BlockSpec((tm, D), lambda i: (i, 0)),
            pl.BlockSpec((tm, D), lambda i: (i, gate_blk)),
            pl.BlockSpec((tm, D), lambda i: (i, gate_blk + 1)),
            pl.BlockSpec((tm, D), lambda i: (i, 0)),
            pl.BlockSpec((D, D), full),
            pl.BlockSpec((D, D), full),
            pl.BlockSpec((D, D), full),
            pl.BlockSpec((1, D), full),
        ],
        out_specs=[
            pl.BlockSpec((tm, D), lambda i: (i, 0)),
            pl.BlockSpec((D, tm), lambda i: (0, i)),
        ],
        out_shape=[
            jax.ShapeDtypeStruct((T, D), F32),
            jax.ShapeDtypeStruct((D, T), BF16),
        ],
        compiler_params=_params("parallel"),
        name="mix",
    )(gla_o, sb_o, proj, proj, x2, wa, wb, wo, g)


def _sort16_network():
    pairs = []

    def merge(lo, n, r):
        step = 2 * r
        if step < n:
            merge(lo, n, step)
            merge(lo + r, n, step)
            pairs.extend((i, i + r) for i in range(lo + r, lo + n - r, step))
        else:
            pairs.append((lo, lo + r))

    def sort(lo, n):
        if n > 1:
            sort(lo, n // 2)
            sort(lo + n // 2, n // 2)
            merge(lo, n, 1)

    sort(0, 16)
    return tuple(pairs)


_SORT16 = _sort16_network()
_BITONIC16 = tuple((i, i + d) for d in (8, 4, 2, 1) for i in range(16) if not i & d)


def _exchange(x, pairs):
    for i, j in pairs:
        x[i], x[j] = jnp.maximum(x[i], x[j]), jnp.minimum(x[i], x[j])


def _merge_sublanes(x):
    for shift in (4, 2, 1):
        y = [pltpu.roll(v, shift, 0) for v in x]
        x = [jnp.maximum(x[i], y[15 - i]) for i in range(16)]
        _exchange(x, _BITONIC16)
    return x


def _route_kernel(xt_ref, wq_ref, k1_ref, k2_ref, cnt_ref, e1_ref, rk_ref, e2_ref, q_scr):
    K = PEER_TOPK
    tt = xt_ref.shape[1]
    sub = lax.broadcasted_iota(jnp.int32, (8, tt), 0)
    q_scr[...] = _dot(wq_ref[...], xt_ref[...]).astype(BF16)

    def head(h, _):
        qt = q_scr[pl.ds(pl.multiple_of(h * 2 * PEER_HALF, 2 * PEER_HALF), 2 * PEER_HALF), :]
        s1 = _dot(k1_ref[h], qt[:PEER_HALF])
        s2 = _dot(k2_ref[h], qt[PEER_HALF:])
        s1 = [s1[8 * k:8 * k + 8] for k in range(16)]
        s2 = [s2[8 * k:8 * k + 8] for k in range(16)]
        v1, v2 = list(s1), list(s2)
        _exchange(v1, _SORT16)
        _exchange(v2, _SORT16)
        v1 = _merge_sublanes(v1)
        v2 = _merge_sublanes(v2)

        lo8, hi8 = v1[0], v1[8]
        for a in range(1, 8):
            lo8 = jnp.where(sub == a, v1[a], lo8)
            hi8 = jnp.where(sub == a, v1[8 + a], hi8)
        p = [lo8 + v2[0]] + [jnp.where(sub < K // (b + 1), lo8 + v2[b], NEG_INF) for b in range(1, K)]
        ph = hi8 + v2[0]
        top = list(p)
        top[K - 1] = jnp.maximum(top[K - 1], ph)
        _exchange(top, _BITONIC16)
        top = _merge_sublanes(top)
        thr = top[K - 1]
        z = jnp.ones_like(thr)
        for i in range(1, K):
            z = z + jnp.exp(top[i] - top[0])
        inv_z = 1.0 / z

        pairs_lo = jnp.where(p[0] >= thr, 1.0, 0.0)
        for b in range(1, K):
            pairs_lo = pairs_lo + jnp.where(p[b] >= thr, 1.0, 0.0)
        pairs_hi = jnp.where(ph >= thr, 1.0, 0.0)
        pairs = [jnp.broadcast_to((pairs_lo if a < 8 else pairs_hi)[a % 8:a % 8 + 1], (8, tt)) for a in range(K)]

        for k in range(0, 16, 2):
            cnt, rk = [], []
            for kk in (k, k + 1):
                c = jnp.zeros((8, tt), F32)
                r = jnp.zeros((8, tt), F32)
                for a in range(K):
                    c = jnp.where(s1[kk] == v1[a], pairs[a], c)
                    r = jnp.where(v2[a] > s2[kk], a + 1.0, r)
                cnt.append(c)
                rk.append(r)
            rows = slice(8 * k, 8 * k + 16)
            cnt_ref[h, rows, :] = jnp.concatenate(cnt, axis=0)
            rk_ref[h, rows, :] = jnp.concatenate(rk, axis=0).astype(BF16)
            e1_ref[h, rows, :] = jnp.concatenate([jnp.exp(s1[k] - v1[0]) * inv_z,
                                                  jnp.exp(s1[k + 1] - v1[0]) * inv_z], axis=0)
            e2_ref[h, rows, :] = jnp.concatenate([jnp.exp(s2[k] - v2[0]),
                                                  jnp.exp(s2[k + 1] - v2[0])], axis=0).astype(BF16)
        return 0

    lax.fori_loop(0, PEER_HEADS, head, 0)


def _route(xt, wq_t, k1, k2):
    D, T = xt.shape
    H, N = PEER_HEADS, PEER_NKEYS
    tt = min(1024, T)
    return pl.pallas_call(
        _route_kernel,
        grid=(T // tt,),
        in_specs=[
            pl.BlockSpec((D, tt), lambda i: (0, i)),
            pl.BlockSpec(wq_t.shape, lambda i: (0, 0)),
            pl.BlockSpec(k1.shape, lambda i: (0, 0, 0)),
            pl.BlockSpec(k2.shape, lambda i: (0, 0, 0)),
        ],
        out_specs=[pl.BlockSpec((H, N, tt), lambda i: (0, 0, i))] * 4,
        out_shape=[
            jax.ShapeDtypeStruct((H, N, T), F32),
            jax.ShapeDtypeStruct((H, N, T), F32),
            jax.ShapeDtypeStruct((H, N, T), BF16),
            jax.ShapeDtypeStruct((H, N, T), BF16),
        ],
        scratch_shapes=[pltpu.VMEM((wq_t.shape[0], tt), BF16)],
        compiler_params=_params("parallel"),
        name="route",
    )(xt, wq_t, k1, k2)


PEER_NI = 4
PEER_TT = 1024


def _peer_kernel(xt_ref, u_ref, vt_ref, cnt_cur, e1_cur, cnt_prev, e1_prev, rk_ref, e2_ref, gk_ref, o_ref,
                 act0, act1, a_cat):
    s = pl.program_id(1)
    N = PEER_NKEYS
    te = PEER_NI * N
    a0 = a_cat.at[0:te]
    a1 = a_cat.at[te:2 * te]
    k3 = gk_ref[0:1, 0:1].astype(BF16)
    k1 = gk_ref[0:1, 1:2].astype(BF16)

    def scores(half, act_ref):
        act_ref[...] = _dot(u_ref[half * te:(half + 1) * te, :], xt_ref[...])

    def gates(cnt_ref, e1_ref, half, n, act_ref, a_ref):
        W = 2 * LANES
        for lg in range(xt_ref.shape[1] // W):
            ls = slice(lg * W, (lg + 1) * W)
            row = slice(half * PEER_NI + n, half * PEER_NI + n + 1)
            g = jnp.zeros((N, W), BF16)
            for h in range(PEER_HEADS):
                cb = jnp.broadcast_to(cnt_ref[h, row, ls], g.shape).astype(BF16)
                eb = jnp.broadcast_to(e1_ref[h, row, ls], g.shape).astype(BF16)
                g = g + jnp.where(rk_ref[h, :, ls] < cb, e2_ref[h, :, ls], jnp.zeros((), BF16)) * eb
            a_ref[n * N:(n + 1) * N, ls] = _gelu_bf16(act_ref[n * N:(n + 1) * N, ls].astype(BF16), k3, k1) * g

    last = pl.num_programs(1) - 1

    @pl.when(s == 0)
    def _():
        o_ref[...] = jnp.zeros_like(o_ref)
        scores(0, act0)
        scores(1, act1)
        for n in range(PEER_NI):
            gates(cnt_cur, e1_cur, 0, n, act0, a0)

    def mix_all():
        o_ref[...] += _dot(vt_ref[...], a_cat[...])

    @pl.when((s > 0) & (s < last))
    def _():
        scores(0, act0)
        for n in range(PEER_NI):
            gates(cnt_prev, e1_prev, 1, n, act1, a1)
        mix_all()
        scores(1, act1)
        for n in range(PEER_NI):
            gates(cnt_cur, e1_cur, 0, n, act0, a0)

    @pl.when(s == last)
    def _():
        for n in range(PEER_NI):
            gates(cnt_prev, e1_prev, 1, n, act1, a1)
        mix_all()


def _peer(xt, u_bf, vt_bf, cnt, e1, rk, e2):
    D, T = xt.shape
    H, N = PEER_HEADS, PEER_NKEYS
    tt = min(PEER_TT, T)
    te = PEER_NI * N
    nblk = (N * N) // (2 * te)
    cur = lambda i, s: (0, jnp.minimum(s, nblk - 1), i)
    prev = lambda i, s: (0, jnp.maximum(s - 1, 0), i)
    return pl.pallas_call(
        _peer_kernel,
        grid=(T // tt, nblk + 1),
        in_specs=[
            pl.BlockSpec((D, tt), lambda i, s: (0, i)),
            pl.BlockSpec((2 * te, D), lambda i, s: (jnp.minimum(s, nblk - 1), 0)),
            pl.BlockSpec((D, 2 * te), lambda i, s: (0, jnp.maximum(s - 1, 0))),
            pl.BlockSpec((H, 2 * PEER_NI, tt), cur),
            pl.BlockSpec((H, 2 * PEER_NI, tt), cur),
            pl.BlockSpec((H, 2 * PEER_NI, tt), prev),
            pl.BlockSpec((H, 2 * PEER_NI, tt), prev),
            pl.BlockSpec((H, N, tt), lambda i, s: (0, 0, i)),
            pl.BlockSpec((H, N, tt), lambda i, s: (0, 0, i)),
            pl.BlockSpec((8, LANES), lambda i, s: (0, 0)),
        ],
        out_specs=pl.BlockSpec((D, tt), lambda i, s: (0, i)),
        out_shape=jax.ShapeDtypeStruct((D, T), F32),
        scratch_shapes=[pltpu.VMEM((te, tt), F32)] * 2 + [pltpu.VMEM((2 * te, tt), BF16)],
        compiler_params=_params("parallel", "arbitrary"),
        name="peer",
    )(xt, u_bf, vt_bf, cnt, e1, cnt, e1, rk, e2, _gelu_consts())


def _vprep_kernel(v_ref, o_ref):
    o_ref[...] = v_ref[...].T.astype(BF16)


def _vprep(v):
    E, D = v.shape
    te = 2048
    return pl.pallas_call(
        _vprep_kernel,
        grid=(E // te,),
        in_specs=[pl.BlockSpec((te, D), lambda i: (i, 0))],
        out_specs=pl.BlockSpec((D, te), lambda i: (0, i)),
        out_shape=jax.ShapeDtypeStruct((D, E), BF16),
        compiler_params=_params("parallel"),
        name="vprep",
    )(v)


def _final_kernel(h_ref, pt_ref, p_ref, wp_ref, wg_ref, gp_ref, gf_ref, o_ref):
    h2 = h_ref[...] + pt_ref[...].T
    ple = _dot(p_ref[...].astype(BF16), wp_ref[...])
    gate = _sigmoid(_dot(_rms(h2, gp_ref[...]).astype(BF16), wg_ref[...]))
    o_ref[...] = _rms(h2 + ple * gate, gf_ref[...])


def _final(h1, peer_t, p2, wp, wg, gp, gf):
    T, D = h1.shape
    tm = min(1024, T)
    full = lambda i: (0, 0)
    return pl.pallas_call(
        _final_kernel,
        grid=(T // tm,),
        in_specs=[
            pl.BlockSpec((tm, D), lambda i: (i, 0)),
            pl.BlockSpec((D, tm), lambda i: (0, i)),
            pl.BlockSpec((tm, PLE_DIM), lambda i: (i, 0)),
            pl.BlockSpec((PLE_DIM, D), full),
            pl.BlockSpec((D, D), full),
            pl.BlockSpec((1, D), full),
            pl.BlockSpec((1, D), full),
        ],
        out_specs=pl.BlockSpec((tm, D), lambda i: (i, 0)),
        out_shape=jax.ShapeDtypeStruct((T, D), F32),
        compiler_params=_params("parallel"),
        name="final",
    )(h1, peer_t, p2, wp, wg, gp, gf)


def kernel(x, p, g_mix, w_in, w_gla_a_up, b_gla_a, g_gla_o, w_gla_out, w_sb_out, w_o, g_ffn, w_peer_q, peer_k1, peer_k2, peer_u, peer_v, g_ple, w_ple_gate, w_ple, g_final):
    B, S, D = x.shape
    T = B * S
    depth = w_in.shape[0]
    assert depth == 1, "the final RMSNorm is fused into the last stage of a single layer"
    qk = GLA_HEADS * GLA_DK
    gv = GLA_HEADS * GLA_DV
    lo = 2 * qk + 2 * gv
    h = x.reshape(T, D)
    for i in range(depth):
        w_main = jnp.concatenate([w_in[i][:, :lo], w_in[i][:, lo + GLA_RANK:]], axis=1).astype(BF16)
        w_ga = jnp.pad(w_in[i][:, lo:lo + GLA_RANK], ((0, 0), (0, LANES - GLA_RANK))).astype(BF16)
        wup = jnp.pad(w_gla_a_up[i], ((0, LANES - GLA_RANK), (0, 0)))

        proj, a_lo = _inproj(h, g_mix[i][None], w_main, w_ga)
        proj3 = proj.reshape(B, S, -1)
        gla_o = _gla(proj3, a_lo.reshape(B, S, LANES), wup, b_gla_a[i][None], g_gla_o[i][None])
        sb_o = _sb(proj3)
        h1, xt = _mix(gla_o.reshape(T, -1), sb_o.reshape(T, -1), proj, h,
                      w_gla_out[i].astype(BF16), w_sb_out[i].astype(BF16), w_o[i].astype(BF16), g_ffn[i][None])
        cnt, e1, rk, e2 = _route(xt, w_peer_q[i].astype(BF16).T, peer_k1[i].astype(BF16), peer_k2[i].astype(BF16))
        peer_t = _peer(xt, peer_u[i].astype(BF16), _vprep(peer_v[i]), cnt, e1, rk, e2)
        h = _final(h1, peer_t, p[i].reshape(T, -1), w_ple[i].astype(BF16), w_ple_gate[i].astype(BF16),
                   g_ple[i][None], g_final[None])
    return h.reshape(B, S, D)
```
